```python
import math
import jax
import jax.numpy as jnp
from jax import lax
import numpy as np


D_MODEL = 1024
BATCH = 4
SEQ = 8192
DEPTH = 4

GRID_W = 64
CTX_LEN = 256
N_EVEN = (DEPTH + 1) // 2
N_ODD = DEPTH // 2
N_MOD = 6
NORM_EPS = 1e-6
ROPE_BASE = 10000.0
Q_BLOCK = 128
NEG_INF = -1e30

A_GROUPS = 4
A_GROUP_DIM = 128
A_WIDTH = A_GROUPS * A_GROUP_DIM
CHUNK = 128
B_HEADS = 4
B_HEAD_DIM = 64
B_WIDTH = B_HEADS * 2 * B_HEAD_DIM
EVEN_IN = 2 * A_WIDTH + 3 * B_WIDTH
C_HEADS = 8
C_HEAD_DIM = 64
C_WIDTH = C_HEADS * C_HEAD_DIM
WIN_H = 8
WIN_W = 16
D_HEADS = 4
D_NOPE = 64
D_ROPE = 32
D_VDIM = 128
Q_LORA = 256
KV_LORA = 128
D_WIDTH = D_HEADS * D_VDIM
ODD_IN = 3 * C_WIDTH + Q_LORA + KV_LORA + D_ROPE
MIX_WIDTH = A_WIDTH + B_WIDTH
D_FF = 2816
N_EXPERTS = 8
TOP_K = 2
D_FF_EXPERT = 3584

kernel_name = 'hybrid_diffusion_trunk'


def rmsnorm(x, g):
    xf = x.astype(jnp.float32)
    y = xf * lax.rsqrt(jnp.mean(xf * xf, axis=-1, keepdims=True) + NORM_EPS)
    return (y * g.astype(jnp.float32)).astype(x.dtype)


def layernorm(x, g):
    xf = x.astype(jnp.float32)
    xc = xf - jnp.mean(xf, axis=-1, keepdims=True)
    y = xc * lax.rsqrt(jnp.mean(xc * xc, axis=-1, keepdims=True) + NORM_EPS)
    return (y * g.astype(jnp.float32)).astype(x.dtype)


def modulate(h, shift, scale):
    return h * (1 + scale) + shift


def axial_rope_tables(row, col, dim):
    n_freq = dim // 4
    inv = 1.0 / (ROPE_BASE ** (jnp.arange(n_freq, dtype=jnp.float32) / n_freq))
    ang = jnp.concatenate([row.astype(jnp.float32)[:, None] * inv, col.astype(jnp.float32)[:, None] * inv], axis=-1)
    return jnp.cos(ang), jnp.sin(ang)


def apply_rope(x, cos, sin):
    xf = x.astype(jnp.float32)
    x1, x2 = jnp.split(xf, 2, axis=-1)
    cs, sn = cos[None, :, None, :], sin[None, :, None, :]
    return jnp.concatenate([x1 * cs - x2 * sn, x2 * cs + x1 * sn], axis=-1).astype(x.dtype)


def blocked_attention(q, k, v, map_w, scale):
    bsz, s_len, heads, n_maps, dk = q.shape
    qb = jnp.moveaxis(q.reshape(bsz, s_len // Q_BLOCK, Q_BLOCK, heads, n_maps, dk), 1, 0)
    w = map_w.astype(jnp.float32)

    def one_block(qi):
        s = jnp.einsum('bqhmd,bkhmd->bhmqk', qi, k).astype(jnp.float32) * scale
        p = jnp.einsum('bhmqk,m->bhqk', jax.nn.softmax(s, axis=-1), w)
        return jnp.einsum('bhqk,bkhd->bqhd', p.astype(v.dtype), v)

    out = lax.map(one_block, qb)
    return jnp.moveaxis(out, 0, 1).reshape(bsz, s_len, heads, v.shape[-1])


def chunk_spatial_gating(a, w_s, b_s, g_v):
    u, v = jnp.split(a, 2, axis=-1)
    v = layernorm(v, g_v)
    bsz, s_len, _ = v.shape
    vc = v.reshape(bsz, s_len // CHUNK, CHUNK, A_GROUPS, A_GROUP_DIM)
    mixed = jnp.einsum('gpq,bcqgd->bcpgd', w_s, vc) + b_s.T[None, None, :, :, None]
    return u * mixed.reshape(bsz, s_len, A_WIDTH)


def neighbourhood_attention(q, k, v, k_ctx, v_ctx, rpb, scale):
    bsz, s_len, heads, d = q.shape
    rows = s_len // GRID_W
    kh = min(WIN_H, rows)
    r = jnp.arange(rows)
    row_start = jnp.clip(r - WIN_H // 2, 0, rows - kh)
    ridx = row_start[:, None] + jnp.arange(kh)[None, :]
    qg = q.reshape(bsz, rows, GRID_W, heads, d)
    kg = k.reshape(bsz, rows, GRID_W, heads, d)[:, ridx].reshape(bsz, rows, kh * GRID_W, heads, d)
    vg = v.reshape(bsz, rows, GRID_W, heads, d)[:, ridx].reshape(bsz, rows, kh * GRID_W, heads, d)
    cq = jnp.arange(GRID_W)
    col_start = jnp.clip(cq - WIN_W // 2, 0, GRID_W - WIN_W)
    in_win = (cq[None, :] >= col_start[:, None]) & (cq[None, :] < col_start[:, None] + WIN_W)
    ro = ridx - r[:, None] + (WIN_H - 1)
    co = jnp.clip(cq[None, :] - cq[:, None] + (WIN_W - 1), 0, 2 * WIN_W - 2)
    bias = rpb[:, ro[:, None, :, None], co[None, :, None, :]]
    bias = bias.astype(jnp.float32).reshape(heads, rows, GRID_W, kh * GRID_W)
    mask = jnp.broadcast_to(in_win[:, None, :], (GRID_W, kh, GRID_W)).reshape(GRID_W, kh * GRID_W)
    bias = jnp.where(mask, bias, NEG_INF)
    s_loc = jnp.einsum('brqhd,brkhd->bhrqk', qg, kg).astype(jnp.float32) * scale + bias[None]
    s_ctx = jnp.einsum('brqhd,bkhd->bhrqk', qg, k_ctx).astype(jnp.float32) * scale
    p = jax.nn.softmax(jnp.concatenate([s_loc, s_ctx], axis=-1), axis=-1)
    n_loc = kh * GRID_W
    o = (jnp.einsum('bhrqk,brkhd->brqhd', p[..., :n_loc].astype(v.dtype), vg)
         + jnp.einsum('bhrqk,bkhd->brqhd', p[..., n_loc:].astype(v.dtype), v_ctx))
    return o.reshape(bsz, s_len, heads, d)


def swiglu(h, w1, w3, w2):
    return (jax.nn.silu(h @ w1) * (h @ w3)) @ w2


def moe_swiglu(h, router_w, w1, w3, w2):
    logits = (h @ router_w).astype(jnp.float32)
    top_val, top_idx = lax.top_k(logits, TOP_K)
    gates = jax.nn.softmax(top_val, axis=-1)
    dense_gate = jnp.sum(jax.nn.one_hot(top_idx, N_EXPERTS, dtype=jnp.float32) * gates[..., None], axis=-2).astype(h.dtype)
    out = jnp.zeros_like(h)
    for e in range(N_EXPERTS):
        out = out + dense_gate[..., e:e + 1] * swiglu(h, w1[e], w3[e], w2[e])
    return out


def even_mixer(h, hc, layer, w_in, w_out, sgu_w, sgu_b, sgu_g, lq1, lk1, lq2, lk2, subln_g, rope, need_ctx):
    lam_init = 0.8 - 0.6 * math.exp(-0.3 * layer)
    lam = (jnp.exp(jnp.sum(lq1.astype(jnp.float32) * lk1.astype(jnp.float32)))
           - jnp.exp(jnp.sum(lq2.astype(jnp.float32) * lk2.astype(jnp.float32))) + lam_init)
    map_w = jnp.stack([jnp.ones_like(lam), -lam])
    scale = 1.0 / math.sqrt(B_HEAD_DIM)

    def qkv(p):
        bsz, s_len = p.shape[:2]
        q, k, v = jnp.split(p[..., 2 * A_WIDTH:], 3, axis=-1)
        return (q.reshape(bsz, s_len, 2 * B_HEADS, B_HEAD_DIM),
                k.reshape(bsz, s_len, 2 * B_HEADS, B_HEAD_DIM),
                v.reshape(bsz, s_len, B_HEADS, 2 * B_HEAD_DIM))

    def maps(t):
        return t.reshape(t.shape[0], t.shape[1], B_HEADS, 2, B_HEAD_DIM)

    def merge(p, q, k, v):
        o = rmsnorm(blocked_attention(maps(q), k, v, map_w, scale), subln_g) * (1 - lam_init)
        y_a = chunk_spatial_gating(jax.nn.gelu(p[..., :2 * A_WIDTH]), sgu_w, sgu_b, sgu_g)
        return jnp.concatenate([y_a, o.reshape(p.shape[0], p.shape[1], B_WIDTH)], axis=-1) @ w_out

    p, pc = h @ w_in, hc @ w_in
    q, k, v = qkv(p)
    qc, kc, vc = qkv(pc)
    cos, sin = rope
    q, k = apply_rope(q, cos, sin), apply_rope(k, cos, sin)
    k_all = jnp.concatenate([maps(kc), maps(k)], axis=1)
    v_all = jnp.concatenate([vc, v], axis=1)
    y = merge(p, q, k_all, v_all)
    yc = merge(pc, qc, maps(kc), vc) if need_ctx else None
    return y, yc


def odd_mixer(h, hc, w_in, w_out, rpb, gq, w_uq, gkv, w_ukv, rope, need_ctx):
    scale_c = 1.0 / math.sqrt(C_HEAD_DIM)
    scale_d = 1.0 / math.sqrt(D_NOPE + D_ROPE)
    one_map = jnp.ones((1,), jnp.float32)
    o_cq = 3 * C_WIDTH
    o_ckv = o_cq + Q_LORA
    o_kr = o_ckv + KV_LORA

    def na_qkv(p):
        bsz, s_len = p.shape[:2]
        q, k, v = jnp.split(p[..., :3 * C_WIDTH], 3, axis=-1)
        return (q.reshape(bsz, s_len, C_HEADS, C_HEAD_DIM), k.reshape(bsz, s_len, C_HEADS, C_HEAD_DIM),
                v.reshape(bsz, s_len, C_HEADS, C_HEAD_DIM))

    def mla_q(p):
        bsz, s_len = p.shape[:2]
        return (rmsnorm(p[..., o_cq:o_ckv], gq) @ w_uq).reshape(bsz, s_len, D_HEADS, D_NOPE + D_ROPE)

    def mla_kv(p):
        bsz, s_len = p.shape[:2]
        kv = (rmsnorm(p[..., o_ckv:o_kr], gkv) @ w_ukv).reshape(bsz, s_len, D_HEADS, D_NOPE + D_VDIM)
        return kv[..., :D_NOPE], kv[..., D_NOPE:], p[..., o_kr:][:, :, None, :]

    def mla_key(k_nope, k_rope):
        return jnp.concatenate([k_nope, jnp.broadcast_to(k_rope, k_nope.shape[:3] + (D_ROPE,))], axis=-1)

    def merge(o_c, o_d):
        bsz, s_len = o_c.shape[:2]
        return jnp.concatenate([o_c.reshape(bsz, s_len, C_WIDTH), o_d.reshape(bsz, s_len, D_WIDTH)], axis=-1) @ w_out

    p, pc = h @ w_in, hc @ w_in
    cos, sin = rope
    q_c, k_c, v_c = na_qkv(p)
    q_cc, k_cc, v_cc = na_qkv(pc)
    q_d = mla_q(p)
    q_d = jnp.concatenate([q_d[..., :D_NOPE], apply_rope(q_d[..., D_NOPE:], cos, sin)], axis=-1)
    kn, v_d, kr = mla_kv(p)
    k_d = mla_key(kn, apply_rope(kr, cos, sin))
    kn_c, v_dc, kr_c = mla_kv(pc)
    k_dc = mla_key(kn_c, kr_c)
    o_c = neighbourhood_attention(q_c, k_c, v_c, k_cc, v_cc, rpb, scale_c)
    o_d = blocked_attention(q_d[:, :, :, None], jnp.concatenate([k_dc, k_d], axis=1)[:, :, :, None],
                            jnp.concatenate([v_dc, v_d], axis=1), one_map, scale_d)
    y = merge(o_c, o_d)
    if need_ctx:
        oc_c = blocked_attention(q_cc[:, :, :, None], k_cc[:, :, :, None], v_cc, one_map, scale_c)
        od_c = blocked_attention(mla_q(pc)[:, :, :, None], k_dc[:, :, :, None], v_dc, one_map, scale_d)
        yc = merge(oc_c, od_c)
    else:
        yc = None
    return y, yc


def setup_inputs(seed: int = 0) -> dict:
    key = jax.random.key(seed)
    ks = jax.random.split(key, 31)
    f32 = jnp.float32

    def nrm(k, shape, scale):
        return jax.random.normal(k, shape, f32) * scale

    def gain(k, shape):
        return 1.0 + 0.05 * jax.random.normal(k, shape, f32)

    return {
        'x': nrm(ks[0], (BATCH, SEQ, D_MODEL), 1.0),
        'c': nrm(ks[1], (BATCH, D_MODEL), 1.0),
        'ctx': nrm(ks[2], (BATCH, CTX_LEN, D_MODEL), 1.0),
        'c_ctx': nrm(ks[3], (D_MODEL,), 1.0),
        'w_mod': nrm(ks[4], (DEPTH, D_MODEL, N_MOD * D_MODEL), 0.5 * D_MODEL ** -0.5),
        'b_mod': nrm(ks[5], (DEPTH, N_MOD * D_MODEL), 0.01),
        'norm_g': gain(ks[6], (DEPTH, 4, D_MODEL)),
        'ev_w_in': nrm(ks[7], (N_EVEN, D_MODEL, EVEN_IN), D_MODEL ** -0.5),
        'ev_w_out': nrm(ks[8], (N_EVEN, MIX_WIDTH, D_MODEL), MIX_WIDTH ** -0.5),
        'sgu_w': nrm(ks[9], (N_EVEN, A_GROUPS, CHUNK, CHUNK), CHUNK ** -0.5),
        'sgu_b': gain(ks[10], (N_EVEN, A_GROUPS, CHUNK)),
        'sgu_g': gain(ks[11], (N_EVEN, A_WIDTH)),
        'diff_lq1': nrm(ks[12], (N_EVEN, B_HEAD_DIM), 0.1),
        'diff_lk1': nrm(ks[13], (N_EVEN, B_HEAD_DIM), 0.1),
        'diff_lq2': nrm(ks[14], (N_EVEN, B_HEAD_DIM), 0.1),
        'diff_lk2': nrm(ks[15], (N_EVEN, B_HEAD_DIM), 0.1),
        'diff_subln_g': gain(ks[16], (N_EVEN, 2 * B_HEAD_DIM)),
        'ffn_w1': nrm(ks[17], (N_EVEN, D_MODEL, D_FF), D_MODEL ** -0.5),
        'ffn_w3': nrm(ks[18], (N_EVEN, D_MODEL, D_FF), D_MODEL ** -0.5),
        'ffn_w2': nrm(ks[19], (N_EVEN, D_FF, D_MODEL), D_FF ** -0.5),
        'od_w_in': nrm(ks[20], (N_ODD, D_MODEL, ODD_IN), D_MODEL ** -0.5),
        'od_w_out': nrm(ks[21], (N_ODD, MIX_WIDTH, D_MODEL), MIX_WIDTH ** -0.5),
        'na_rpb': nrm(ks[22], (N_ODD, C_HEADS, 2 * WIN_H - 1, 2 * WIN_W - 1), 0.2),
        'mla_gq': gain(ks[23], (N_ODD, Q_LORA)),
        'mla_w_uq': nrm(ks[24], (N_ODD, Q_LORA, D_HEADS * (D_NOPE + D_ROPE)), Q_LORA ** -0.5),
        'mla_gkv': gain(ks[25], (N_ODD, KV_LORA)),
        'mla_w_ukv': nrm(ks[26], (N_ODD, KV_LORA, D_HEADS * (D_NOPE + D_VDIM)), KV_LORA ** -0.5),
        'router_w': nrm(ks[27], (N_ODD, D_MODEL, N_EXPERTS), D_MODEL ** -0.5),
        'moe_w1': nrm(ks[28], (N_ODD, N_EXPERTS, D_MODEL, D_FF_EXPERT), D_MODEL ** -0.5),
        'moe_w3': nrm(ks[29], (N_ODD, N_EXPERTS, D_MODEL, D_FF_EXPERT), D_MODEL ** -0.5),
        'moe_w2': nrm(ks[30], (N_ODD, N_EXPERTS, D_FF_EXPERT, D_MODEL), D_FF_EXPERT ** -0.5),
    }


def reference(x, c, ctx, c_ctx, w_mod, b_mod, norm_g, ev_w_in, ev_w_out, sgu_w, sgu_b, sgu_g,
              diff_lq1, diff_lk1, diff_lq2, diff_lk2, diff_subln_g, ffn_w1, ffn_w3, ffn_w2,
              od_w_in, od_w_out, na_rpb, mla_gq, mla_w_uq, mla_gkv, mla_w_ukv,
              router_w, moe_w1, moe_w3, moe_w2):
    bsz, s_len, _ = x.shape
    t = jnp.arange(s_len, dtype=jnp.int32)
    row, col = t // GRID_W, t % GRID_W
    rope_b = axial_rope_tables(row, col, B_HEAD_DIM)
    rope_d = axial_rope_tables(row, col, D_ROPE)
    silu_c = jax.nn.silu(c)
    silu_cc = jax.nn.silu(c_ctx)
    for l in range(DEPTH):
        i = l // 2
        need_ctx = l < DEPTH - 1
        mod = (silu_c @ w_mod[l] + b_mod[l]).reshape(bsz, N_MOD, 1, D_MODEL)
        mod_c = (silu_cc @ w_mod[l] + b_mod[l]).reshape(N_MOD, D_MODEL)
        sh1, sc1, g1, sh2, sc2, g2 = [mod[:, j] for j in range(N_MOD)]
        ch1, cs1, cg1, ch2, cs2, cg2 = [mod_c[j] for j in range(N_MOD)]
        h = modulate(rmsnorm(x, norm_g[l, 0]), sh1, sc1)
        hc = modulate(rmsnorm(ctx, norm_g[l, 0]), ch1, cs1)
        if l % 2 == 0:
            y, yc = even_mixer(h, hc, l, ev_w_in[i], ev_w_out[i], sgu_w[i], sgu_b[i], sgu_g[i],
                               diff_lq1[i], diff_lk1[i], diff_lq2[i], diff_lk2[i], diff_subln_g[i],
                               rope_b, need_ctx)

            def ffn(z):
                return swiglu(z, ffn_w1[i], ffn_w3[i], ffn_w2[i])
        else:
            y, yc = odd_mixer(h, hc, od_w_in[i], od_w_out[i], na_rpb[i], mla_gq[i], mla_w_uq[i],
                              mla_gkv[i], mla_w_ukv[i], rope_d, need_ctx)

            def ffn(z):
                return moe_swiglu(z, router_w[i], moe_w1[i], moe_w3[i], moe_w2[i])
        x = x + g1 * rmsnorm(y, norm_g[l, 1])
        h = modulate(rmsnorm(x, norm_g[l, 2]), sh2, sc2)
        x = x + g2 * rmsnorm(ffn(h), norm_g[l, 3])
        if need_ctx:
            ctx = ctx + cg1 * rmsnorm(yc, norm_g[l, 1])
            hc = modulate(rmsnorm(ctx, norm_g[l, 2]), ch2, cs2)
            ctx = ctx + cg2 * rmsnorm(ffn(hc), norm_g[l, 3])
    return x
```

```python
import functools
import math

import jax
import jax.numpy as jnp
from jax import lax
from jax.experimental import pallas as pl
from jax.experimental.pallas import tpu as pltpu

F32 = jnp.float32
BF16 = jnp.bfloat16

GRID_W = 64
CTX_LEN = 256
N_MOD = 6
NORM_EPS = 1e-6
ROPE_BASE = 10000.0
NEG_INF = -1e30
A_GROUPS = 4
A_WIDTH = 512
CHUNK = 128
B_HEADS = 4
B_HEAD_DIM = 64
B_WIDTH = 512
C_HEADS = 8
C_HEAD_DIM = 64
C_WIDTH = 512
WIN_H = 8
WIN_W = 16
D_HEADS = 4
D_NOPE = 64
D_ROPE = 32
D_VDIM = 128
Q_LORA = 256
KV_LORA = 128
N_EXPERTS = 8
LOG2E = 1.4426950408889634

LANES = 128
SUBLANES = 8
VMEM_LIMIT = 56 * 1024 * 1024

NA_GROUP_ROWS = 4
NA_WIN_ROWS = WIN_H + NA_GROUP_ROWS - 1
EXPERT_TILE = 512


def _cparams(sem):
    return pltpu.CompilerParams(dimension_semantics=sem, vmem_limit_bytes=VMEM_LIMIT)


def _dot(a, b):
    return jnp.dot(a, b, preferred_element_type=F32)


def _dot_nt(a, b):
    return lax.dot_general(a, b, (((1,), (1,)), ((), ())), preferred_element_type=F32)


def _rms(x, g):
    return x * lax.rsqrt(jnp.mean(x * x, axis=-1, keepdims=True) + NORM_EPS) * g


def _gelu_tanh(x):
    return 0.5 * x * (1.0 + jnp.tanh(math.sqrt(2.0 / math.pi) * (x + 0.044715 * (x * x * x))))


def _silu(x):
    return x / (1.0 + jnp.exp(-x))


def _is_ctx_rows(tm, tiles_per_batch):
    i = pl.program_id(0)
    row = (i % tiles_per_batch) * tm + lax.broadcasted_iota(jnp.int32, (tm, 1), 0)
    return row < CTX_LEN


def _mod_row(mod_ref, k, is_ctx):
    return jnp.where(is_ctx, mod_ref[SUBLANES + k:SUBLANES + k + 1, :], mod_ref[k:k + 1, :])


def _rope_block(xb, c, s1, s2, shift):
    return xb * c + pltpu.roll(xb, LANES - shift, 1) * s1 + pltpu.roll(xb, shift, 1) * s2


def _to_token_tiles(dst_ref, val, rows):
    for s in range(val.shape[1] // LANES):
        dst_ref[pl.ds(s, rows, stride=SUBLANES), :] = val[:, s * LANES:(s + 1) * LANES]


def _from_token_tiles(src_ref, rows, n_seg):
    return jnp.concatenate(
        [src_ref[pl.ds(s, rows, stride=SUBLANES), :] for s in range(n_seg)], axis=1)


def _mod_kernel(s_ref, w_ref, b_ref, o_ref):
    s = _silu(s_ref[...])
    o_ref[...] = _dot(s.astype(BF16), w_ref[...].astype(BF16)) + b_ref[...]


def _modulation(cond, w_mod, b_mod):
    depth, d, _ = w_mod.shape
    return pl.pallas_call(
        _mod_kernel,
        out_shape=jax.ShapeDtypeStruct((depth, SUBLANES, N_MOD * d), F32),
        grid=(depth, N_MOD),
        in_specs=[
            pl.BlockSpec((SUBLANES, d), lambda l, j: (0, 0)),
            pl.BlockSpec((None, d, d), lambda l, j: (l, 0, j)),
            pl.BlockSpec((None, 1, d), lambda l, j: (l, 0, j)),
        ],
        out_specs=pl.BlockSpec((None, SUBLANES, d), lambda l, j: (l, 0, j)),
        compiler_params=_cparams(("parallel", "parallel")),
        name="modulation",
    )(cond, w_mod, b_mod.reshape(depth, 1, N_MOD * d))


def _pre_even_kernel(x_ref, mod_ref, g_ref, w_ref, sw_ref, sb_ref, sg_ref, c_ref, s1_ref, s2_ref,
                     o_ref, *, tm, tiles_per_batch, qscale):
    is_ctx = _is_ctx_rows(tm, tiles_per_batch)
    n = _rms(x_ref[...], g_ref[...])
    h = (n * (1.0 + _mod_row(mod_ref, 1, is_ctx)) + _mod_row(mod_ref, 0, is_ctx)).astype(BF16)

    u = _gelu_tanh(_dot(h, w_ref[:, 0:A_WIDTH]))
    v = _gelu_tanh(_dot(h, w_ref[:, A_WIDTH:2 * A_WIDTH]))
    vc = v - jnp.mean(v, axis=-1, keepdims=True)
    vn = (vc * lax.rsqrt(jnp.mean(vc * vc, axis=-1, keepdims=True) + NORM_EPS) * sg_ref[...]).astype(BF16)
    gd = A_WIDTH // A_GROUPS
    for c in range(tm // CHUNK):
        rs = slice(c * CHUNK, (c + 1) * CHUNK)
        for g in range(A_GROUPS):
            cs = slice(g * gd, (g + 1) * gd)
            mixed = _dot(sw_ref[g], vn[rs, cs]) + sb_ref[:, g:g + 1]
            o_ref[rs, cs] = (u[rs, cs] * mixed).astype(BF16)

    c, s1, s2 = c_ref[...], s1_ref[...], s2_ref[...]
    q = _dot(h, w_ref[:, 2 * A_WIDTH:2 * A_WIDTH + B_WIDTH])
    k = _dot(h, w_ref[:, 2 * A_WIDTH + B_WIDTH:2 * A_WIDTH + 2 * B_WIDTH])
    for j in range(B_WIDTH // LANES):
        ls = slice(j * LANES, (j + 1) * LANES)
        o_ref[:, A_WIDTH + j * LANES:A_WIDTH + (j + 1) * LANES] = (
            _rope_block(q[:, ls], c, s1, s2, B_HEAD_DIM // 2) * qscale).astype(BF16)
        o_ref[:, A_WIDTH + B_WIDTH + j * LANES:A_WIDTH + B_WIDTH + (j + 1) * LANES] = (
            _rope_block(k[:, ls], c, s1, s2, B_HEAD_DIM // 2)).astype(BF16)
    o_ref[:, A_WIDTH + 2 * B_WIDTH:] = _dot(h, w_ref[:, 2 * A_WIDTH + 2 * B_WIDTH:]).astype(BF16)


def _pre_even(x, mod, g0, w_in, sgu_w, sgu_bt, sgu_g, rope, *, tm, tiles_per_batch):
    n, d = x.shape
    t = tiles_per_batch * tm
    wout = A_WIDTH + 3 * B_WIDTH
    const = lambda i: (0, 0)
    kern = functools.partial(_pre_even_kernel, tm=tm, tiles_per_batch=tiles_per_batch,
                             qscale=LOG2E / math.sqrt(B_HEAD_DIM))
    rope_spec = pl.BlockSpec((tm, LANES), lambda i: (i % tiles_per_batch, 0))
    return pl.pallas_call(
        kern,
        out_shape=jax.ShapeDtypeStruct((n, wout), BF16),
        grid=(n // tm,),
        in_specs=[
            pl.BlockSpec((tm, d), lambda i: (i, 0)),
            pl.BlockSpec((None, 2 * SUBLANES, d), lambda i: (i // tiles_per_batch, 0, 0)),
            pl.BlockSpec((1, d), const),
            pl.BlockSpec(w_in.shape, const),
            pl.BlockSpec(sgu_w.shape, lambda i: (0, 0, 0)),
            pl.BlockSpec(sgu_bt.shape, const),
            pl.BlockSpec(sgu_g.shape, const),
            rope_spec, rope_spec, rope_spec,
        ],
        out_specs=pl.BlockSpec((tm, wout), lambda i: (i, 0)),
        compiler_params=_cparams(("parallel",)),
        name="pre_even",
    )(x, mod, g0, w_in, sgu_w, sgu_bt, sgu_g, *rope)


def _pre_odd_kernel(x_ref, mod_ref, g_ref, w_ref, gq_ref, gkv_ref, wuq_ref, wk_ref, wv_ref,
                    c_ref, s1_ref, s2_ref, o_ref, *, tm, tiles_per_batch, scale_c, scale_d):
    is_ctx = _is_ctx_rows(tm, tiles_per_batch)
    n = _rms(x_ref[...], g_ref[...])
    h = (n * (1.0 + _mod_row(mod_ref, 1, is_ctx)) + _mod_row(mod_ref, 0, is_ctx)).astype(BF16)

    o_ref[:, 0:C_WIDTH] = (_dot(h, w_ref[:, 0:C_WIDTH]) * scale_c).astype(BF16)
    o_ref[:, C_WIDTH:3 * C_WIDTH] = _dot(h, w_ref[:, C_WIDTH:3 * C_WIDTH]).astype(BF16)

    low = _dot(h, w_ref[:, 3 * C_WIDTH:])
    cq = _rms(low[:, 0:Q_LORA], gq_ref[...]).astype(BF16)
    ckv = _rms(low[:, Q_LORA:Q_LORA + KV_LORA], gkv_ref[...]).astype(BF16)
    kr = low[:, Q_LORA + KV_LORA:].astype(BF16)
    qd = _dot(cq, wuq_ref[...])
    kd = _dot(jnp.concatenate([ckv, kr], axis=1), wk_ref[...])
    vd = _dot(ckv, wv_ref[...])
    c, s1, s2 = c_ref[...], s1_ref[...], s2_ref[...]
    base = 3 * C_WIDTH
    hw = D_HEADS * LANES
    for j in range(D_HEADS):
        ls = slice(j * LANES, (j + 1) * LANES)
        o_ref[:, base + j * LANES:base + (j + 1) * LANES] = (
            _rope_block(qd[:, ls], c, s1, s2, D_ROPE // 2) * scale_d).astype(BF16)
        o_ref[:, base + hw + j * LANES:base + hw + (j + 1) * LANES] = (
            _rope_block(kd[:, ls], c, s1, s2, D_ROPE // 2)).astype(BF16)
    o_ref[:, base + 2 * hw:] = vd.astype(BF16)


def _pre_odd(x, mod, g0, w_in, gq, gkv, wuq, wk, wv, rope, *, tm, tiles_per_batch):
    n, d = x.shape
    wout = 3 * C_WIDTH + 3 * D_HEADS * LANES
    const = lambda i: (0, 0)
    kern = functools.partial(_pre_odd_kernel, tm=tm, tiles_per_batch=tiles_per_batch,
                             scale_c=LOG2E / math.sqrt(C_HEAD_DIM),
                             scale_d=LOG2E / math.sqrt(D_NOPE + D_ROPE))
    rope_spec = pl.BlockSpec((tm, LANES), lambda i: (i % tiles_per_batch, 0))
    return pl.pallas_call(
        kern,
        out_shape=jax.ShapeDtypeStruct((n, wout), BF16),
        grid=(n // tm,),
        in_specs=[
            pl.BlockSpec((tm, d), lambda i: (i, 0)),
            pl.BlockSpec((None, 2 * SUBLANES, d), lambda i: (i // tiles_per_batch, 0, 0)),
            pl.BlockSpec((1, d), const),
            pl.BlockSpec(w_in.shape, const),
            pl.BlockSpec(gq.shape, const),
            pl.BlockSpec(gkv.shape, const),
            pl.BlockSpec(wuq.shape, const),
            pl.BlockSpec(wk.shape, const),
            pl.BlockSpec(wv.shape, const),
            rope_spec, rope_spec, rope_spec,
        ],
        out_specs=pl.BlockSpec((tm, wout), lambda i: (i, 0)),
        compiler_params=_cparams(("parallel",)),
        name="pre_odd",
    )(x, mod, g0, w_in, gq, gkv, wuq, wk, wv, *rope)


def _attn_kernel(lam_ref, q_ref, k_ref, v_ref, g_ref, o_ref, vt_ref, *, n_maps, tq, tk, n_chunks,
                 post_norm, out_scale):
    qi = pl.program_id(2)

    @pl.when(qi == 0)
    def _():
        eye = (lax.broadcasted_iota(jnp.int32, (LANES, LANES), 0)
               == lax.broadcasted_iota(jnp.int32, (LANES, LANES), 1)).astype(BF16)
        for c in range(n_chunks):
            vt_ref[c] = _dot_nt(eye, v_ref[c * tk:(c + 1) * tk, :]).astype(BF16)

    q = q_ref[...]
    if n_maps == 2:
        lane = lax.broadcasted_iota(jnp.int32, (1, LANES), 1)
        half = LANES // 2
        qs = [jnp.where(lane < half, q, jnp.zeros_like(q)), jnp.where(lane >= half, q, jnp.zeros_like(q))]
    else:
        qs = [q]

    def step(ks, vt, carry):
        new = []
        for qm, (m, l, acc) in zip(qs, carry):
            st = _dot_nt(ks, qm)
            m_new = jnp.maximum(m, jnp.max(st, axis=0, keepdims=True))
            alpha = jnp.exp2(m - m_new)
            p = jnp.exp2(st - m_new)
            l_new = alpha * l + jnp.sum(p, axis=0, keepdims=True)
            acc_new = alpha * acc + _dot(vt, p.astype(BF16))
            new.append((m_new, l_new, acc_new))
        return tuple(new)

    init = tuple((jnp.full((1, tq), NEG_INF, F32), jnp.zeros((1, tq), F32), jnp.zeros((LANES, tq), F32))
                 for _ in qs)

    def finish(carry):
        ot = carry[0][2] * (1.0 / carry[0][1])
        if n_maps == 2:
            ot = ot - lam_ref[0] * (carry[1][2] * (1.0 / carry[1][1]))
        o = ot.T
        if post_norm:
            o = _rms(o, g_ref[...]) * out_scale
        o_ref[...] = o.astype(o_ref.dtype)

    @pl.when(qi == 0)
    def _():
        finish(step(k_ref[0:CTX_LEN, :], vt_ref[0, :, 0:CTX_LEN], init))

    @pl.when(qi > 0)
    def _():
        def body(c, carry):
            start = pl.multiple_of(c * tk, tk)
            return step(k_ref[pl.ds(start, tk), :], vt_ref[c], carry)
        finish(lax.fori_loop(0, n_chunks, body, init))


def _key_chunk(t):
    for tk in (1408, 768, 1280, 512, 256):
        if t % tk == 0:
            return tk
    raise ValueError(f"unsupported sequence length {t}")


def _attention(p3, lam, gain, *, n_heads, n_maps, qcol, kcol, vcol, post_norm, out_scale):
    b, t, _ = p3.shape
    tq = CTX_LEN
    tk = _key_chunk(t)
    kern = functools.partial(_attn_kernel, n_maps=n_maps, tq=tq, tk=tk, n_chunks=t // tk,
                             post_norm=post_norm, out_scale=out_scale)
    return pl.pallas_call(
        kern,
        out_shape=jax.ShapeDtypeStruct((b, t, n_heads * LANES), BF16),
        grid=(b, n_heads, t // tq),
        in_specs=[
            pl.BlockSpec(memory_space=pltpu.SMEM),
            pl.BlockSpec((None, tq, LANES), lambda bi, h, qi: (bi, qi, qcol + h)),
            pl.BlockSpec((None, t, LANES), lambda bi, h, qi: (bi, 0, kcol + h)),
            pl.BlockSpec((None, t, LANES), lambda bi, h, qi: (bi, 0, vcol + h)),
            pl.BlockSpec((1, LANES), lambda bi, h, qi: (0, 0)),
        ],
        out_specs=pl.BlockSpec((None, tq, LANES), lambda bi, h, qi: (bi, qi, h)),
        scratch_shapes=[pltpu.VMEM((t // tk, LANES, tk), BF16)],
        compiler_params=_cparams(("parallel", "parallel", "arbitrary")),
        name="attention_maps%d" % n_maps,
    )(lam, p3, p3, p3, gain)


def _na_kernel(q_ref, k_ref, v_ref, bias_ref, o_ref, *, tq, n_groups, grid_rows):
    qi = pl.program_id(2)
    q = q_ref[...]
    lane = lax.broadcasted_iota(jnp.int32, (1, LANES), 1)
    half = LANES // 2
    kctx = k_ref[0:CTX_LEN, :]
    vctx = v_ref[0:CTX_LEN, :]
    win = NA_WIN_ROWS * GRID_W

    def head_out(hl, local):
        qm = jnp.where((lane < half) if hl == 0 else (lane >= half), q, jnp.zeros_like(q))
        s_c = _dot_nt(qm, kctx)
        m = jnp.max(s_c, axis=-1, keepdims=True)
        if local is not None:
            kloc, vloc, tau = local
            s_l = _dot_nt(qm, kloc) + bias_ref[tau, hl]
            m = jnp.maximum(m, jnp.max(s_l, axis=-1, keepdims=True))
        p_c = jnp.exp2(s_c - m)
        l = jnp.sum(p_c, axis=-1, keepdims=True)
        o = _dot(p_c.astype(BF16), vctx)
        if local is not None:
            p_l = jnp.exp2(s_l - m)
            l = l + jnp.sum(p_l, axis=-1, keepdims=True)
            o = o + _dot(p_l.astype(BF16), vloc)
        return o * (1.0 / l)

    def write(local):
        o_ref[...] = jnp.where(lane < half, head_out(0, local), head_out(1, local)).astype(o_ref.dtype)

    @pl.when(qi == 0)
    def _():
        write(None)

    @pl.when(qi > 0)
    def _():
        g = qi - 1
        krow = jnp.clip(NA_GROUP_ROWS * g - WIN_H // 2, 0, grid_rows - NA_WIN_ROWS)
        start = pl.multiple_of(CTX_LEN + GRID_W * krow, GRID_W)
        tau = jnp.where(g == 0, 0, jnp.where(g == n_groups - 1, 2, 1))
        write((k_ref[pl.ds(start, win), :], v_ref[pl.ds(start, win), :], tau))


def _na_bias_tables(rpb, grid_rows):
    n_groups = grid_rows // NA_GROUP_ROWS
    tabs = []
    for g in (0, 1, n_groups - 1):
        r0 = NA_GROUP_ROWS * g
        krow0 = min(max(r0 - WIN_H // 2, 0), grid_rows - NA_WIN_ROWS)
        r = r0 + jnp.arange(NA_GROUP_ROWS)
        rs = jnp.clip(r - WIN_H // 2, 0, grid_rows - WIN_H)
        kr = krow0 + jnp.arange(NA_WIN_ROWS)
        row_ok = (kr[None, :] >= rs[:, None]) & (kr[None, :] < rs[:, None] + WIN_H)
        ro = jnp.clip(kr[None, :] - r[:, None] + (WIN_H - 1), 0, 2 * WIN_H - 2)
        cq = jnp.arange(GRID_W)
        cs = jnp.clip(cq - WIN_W // 2, 0, GRID_W - WIN_W)
        col_ok = (cq[None, :] >= cs[:, None]) & (cq[None, :] < cs[:, None] + WIN_W)
        co = jnp.clip(cq[None, :] - cq[:, None] + (WIN_W - 1), 0, 2 * WIN_W - 2)
        bias = rpb[:, ro[:, None, :, None], co[None, :, None, :]]
        ok = row_ok[:, None, :, None] & col_ok[None, :, None, :]
        bias = jnp.where(ok[None], bias.astype(F32) * LOG2E, NEG_INF)
        tabs.append(bias.reshape(rpb.shape[0], NA_GROUP_ROWS * GRID_W, NA_WIN_ROWS * GRID_W))
    return jnp.stack(tabs)


def _neighbourhood_attention(p3, bias):
    b, t, _ = p3.shape
    tq = NA_GROUP_ROWS * GRID_W
    grid_rows = (t - CTX_LEN) // GRID_W
    n_groups = grid_rows // NA_GROUP_ROWS
    n_pairs = C_WIDTH // LANES
    kern = functools.partial(_na_kernel, tq=tq, n_groups=n_groups, grid_rows=grid_rows)
    return pl.pallas_call(
        kern,
        out_shape=jax.ShapeDtypeStruct((b, t, C_WIDTH), BF16),
        grid=(b, n_pairs, t // tq),
        in_specs=[
            pl.BlockSpec((None, tq, LANES), lambda bi, h, qi: (bi, qi, h)),
            pl.BlockSpec((None, t, LANES), lambda bi, h, qi: (bi, 0, n_pairs + h)),
            pl.BlockSpec((None, t, LANES), lambda bi, h, qi: (bi, 0, 2 * n_pairs + h)),
            pl.BlockSpec((3, 2, tq, NA_WIN_ROWS * GRID_W), lambda bi, h, qi: (0, h, 0, 0)),
        ],
        out_specs=pl.BlockSpec((None, tq, LANES), lambda bi, h, qi: (bi, qi, h)),
        compiler_params=_cparams(("parallel", "parallel", "arbitrary")),
        name="neighbourhood_attention",
    )(p3, p3, p3, bias)


def _post_common(x_ref, a_ref, b_ref, w_ref, mod_ref, g1_ref, g2_ref, is_ctx):
    wa = a_ref.shape[1]
    y = _dot(a_ref[...], w_ref[0:wa, :]) + _dot(b_ref[...], w_ref[wa:, :])
    x1 = x_ref[...] + _mod_row(mod_ref, 2, is_ctx) * _rms(y, g1_ref[...])
    h2 = _rms(x1, g2_ref[...]) * (1.0 + _mod_row(mod_ref, 4, is_ctx)) + _mod_row(mod_ref, 3, is_ctx)
    return x1, h2


def _post_even_kernel(x_ref, a_ref, b_ref, w_ref, mod_ref, g1_ref, g2_ref, x1_ref, h_ref, *,
                      tm, tiles_per_batch):
    is_ctx = _is_ctx_rows(tm, tiles_per_batch)
    x1, h2 = _post_common(x_ref, a_ref, b_ref, w_ref, mod_ref, g1_ref, g2_ref, is_ctx)
    x1_ref[...] = x1
    h_ref[...] = h2.astype(BF16)


def _post_odd_kernel(x_ref, a_ref, b_ref, w_ref, mod_ref, g1_ref, g2_ref, wr_ref,
                     x1_ref, ht_ref, route_ref, *, tm, tiles_per_batch):
    is_ctx = _is_ctx_rows(tm, tiles_per_batch)
    x1, h2 = _post_common(x_ref, a_ref, b_ref, w_ref, mod_ref, g1_ref, g2_ref, is_ctx)
    x1_ref[...] = x1
    _to_token_tiles(ht_ref, h2, tm)

    lane = lax.broadcasted_iota(jnp.int32, (tm, LANES), 1)
    logits = jnp.where(lane < N_EXPERTS, _dot(h2.astype(BF16), wr_ref[...]), NEG_INF)
    m1 = jnp.max(logits, axis=-1, keepdims=True)
    i1 = jnp.min(jnp.where(logits == m1, lane, LANES), axis=-1, keepdims=True)
    rest = jnp.where(lane == i1, NEG_INF, logits)
    m2 = jnp.max(rest, axis=-1, keepdims=True)
    i2 = jnp.min(jnp.where(rest == m2, lane, LANES), axis=-1, keepdims=True)
    e = jnp.exp(m2 - m1)
    gate1 = 1.0 / (1.0 + e)
    gate2 = e * gate1
    route_ref[...] = jnp.where(
        lane == 0, i1.astype(F32),
        jnp.where(lane == 1, i2.astype(F32),
                  jnp.where(lane == 2, gate1, jnp.where(lane == 3, gate2, 0.0))))


def _post(x, a, b, w_out, mod, g1, g2, router=None, *, a_col, tm, tiles_per_batch):
    n, d = x.shape
    wa, wb = w_out.shape[0] - b.shape[1], b.shape[1]
    const = lambda i: (0, 0)
    in_specs = [
        pl.BlockSpec((tm, d), lambda i: (i, 0)),
        pl.BlockSpec((tm, wa), lambda i: (i, a_col)),
        pl.BlockSpec((tm, wb), lambda i: (i, 0)),
        pl.BlockSpec(w_out.shape, const),
        pl.BlockSpec((None, 2 * SUBLANES, d), lambda i: (i // tiles_per_batch, 0, 0)),
        pl.BlockSpec((1, d), const),
        pl.BlockSpec((1, d), const),
    ]
    row_spec = pl.BlockSpec((tm, d), lambda i: (i, 0))
    if router is None:
        kern = functools.partial(_post_even_kernel, tm=tm, tiles_per_batch=tiles_per_batch)
        return pl.pallas_call(
            kern,
            out_shape=(jax.ShapeDtypeStruct((n, d), F32), jax.ShapeDtypeStruct((n, d), BF16)),
            grid=(n // tm,),
            in_specs=in_specs,
            out_specs=(row_spec, row_spec),
            compiler_params=_cparams(("parallel",)),
            name="post_even",
        )(x, a, b, w_out, mod, g1, g2)
    kern = functools.partial(_post_odd_kernel, tm=tm, tiles_per_batch=tiles_per_batch)
    seg = d // LANES
    return pl.pallas_call(
        kern,
        out_shape=(jax.ShapeDtypeStruct((n, d), F32),
                   jax.ShapeDtypeStruct((n * seg, LANES), F32),
                   jax.ShapeDtypeStruct((n, LANES), F32)),
        grid=(n // tm,),
        in_specs=in_specs + [pl.BlockSpec(router.shape, const)],
        out_specs=(row_spec,
                   pl.BlockSpec((tm * seg, LANES), lambda i: (i, 0)),
                   pl.BlockSpec((tm, LANES), lambda i: (i, 0))),
        compiler_params=_cparams(("parallel",)),
        name="post_odd",
    )(x, a, b, w_out, mod, g1, g2, router)


def _ffn_kernel(x_ref, h_ref, w1_ref, w3_ref, w2_ref, mod_ref, g_ref, o_ref, *, tm, tiles_per_batch,
                n_split):
    is_ctx = _is_ctx_rows(tm, tiles_per_batch)
    h = h_ref[...]
    fc = w1_ref.shape[1] // n_split
    y = None
    for j in range(n_split):
        cs = slice(j * fc, (j + 1) * fc)
        mid = (_silu(_dot(h, w1_ref[:, cs])) * _dot(h, w3_ref[:, cs])).astype(BF16)
        part = _dot(mid, w2_ref[cs, :])
        y = part if y is None else y + part
    o_ref[...] = x_ref[...] + _mod_row(mod_ref, 5, is_ctx) * _rms(y, g_ref[...])


def _ffn(x, h, w1, w3, w2, mod, g3, *, tm, tiles_per_batch):
    n, d = x.shape
    const = lambda i: (0, 0)
    ff = w1.shape[1]
    n_split = 2 if ff % (2 * LANES) == 0 else 1
    kern = functools.partial(_ffn_kernel, tm=tm, tiles_per_batch=tiles_per_batch, n_split=n_split)
    row_spec = pl.BlockSpec((tm, d), lambda i: (i, 0))
    return pl.pallas_call(
        kern,
        out_shape=jax.ShapeDtypeStruct((n, d), F32),
        grid=(n // tm,),
        in_specs=[
            row_spec, row_spec,
            pl.BlockSpec(w1.shape, const, pipeline_mode=pl.Buffered(1)),
            pl.BlockSpec(w3.shape, const, pipeline_mode=pl.Buffered(1)),
            pl.BlockSpec(w2.shape, const, pipeline_mode=pl.Buffered(1)),
            pl.BlockSpec((None, 2 * SUBLANES, d), lambda i: (i // tiles_per_batch, 0, 0)),
            pl.BlockSpec((1, d), const),
        ],
        out_specs=row_spec,
        compiler_params=_cparams(("parallel",)),
        name="dense_ffn",
    )(x, h, w1, w3, w2, mod, g3)


def _rank_kernel(route_ref, tri_ref, rank_ref, count_ref, carry_ref, *, tm):
    i = pl.program_id(0)

    @pl.when(i == 0)
    def _():
        carry_ref[...] = jnp.zeros_like(carry_ref)

    lane = lax.broadcasted_iota(jnp.int32, (tm, LANES), 1)
    r = route_ref[...]
    oh1 = (lane == r[:, 0:1].astype(jnp.int32)).astype(F32)
    oh2 = (lane == r[:, 1:2].astype(jnp.int32)).astype(F32)
    both = oh1 + oh2
    before = _dot(tri_ref[...], both.astype(BF16)) + carry_ref[0:1, :]
    rank1 = jnp.sum(oh1 * before, axis=-1, keepdims=True)
    rank2 = jnp.sum(oh2 * before, axis=-1, keepdims=True)
    rank_ref[...] = jnp.where(lane == 0, rank1, jnp.where(lane == 1, rank2, 0.0))
    carry_ref[...] = carry_ref[...] + jnp.sum(both, axis=0, keepdims=True)
    count_ref[...] = carry_ref[...]


def _rank(route, *, tm):
    n = route.shape[0]
    tri = (jnp.arange(tm)[:, None] > jnp.arange(tm)[None, :]).astype(BF16)
    return pl.pallas_call(
        functools.partial(_rank_kernel, tm=tm),
        out_shape=(jax.ShapeDtypeStruct((n, LANES), F32), jax.ShapeDtypeStruct((SUBLANES, LANES), F32)),
        grid=(n // tm,),
        in_specs=[pl.BlockSpec((tm, LANES), lambda i: (i, 0)),
                  pl.BlockSpec((tm, tm), lambda i: (0, 0))],
        out_specs=(pl.BlockSpec((tm, LANES), lambda i: (i, 0)),
                   pl.BlockSpec((SUBLANES, LANES), lambda i: (0, 0))),
        scratch_shapes=[pltpu.VMEM((SUBLANES, LANES), F32)],
        compiler_params=_cparams(("arbitrary",)),
        name="expert_rank",
    )(route, tri)


def _row_copy(src, dst, src_row, dst_row, sem):
    return pltpu.make_async_copy(src.at[pl.ds(src_row * SUBLANES, SUBLANES), :],
                                 dst.at[pl.ds(dst_row * SUBLANES, SUBLANES), :], sem)


def _dispatch_kernel(slot_ref, pad_ref, h_ref, o_ref, zero_ref, sem, zsem, *, tm, n_tokens):
    i = pl.program_id(0)
    base = i * tm

    def issue(t, _):
        tok = base + t
        _row_copy(h_ref, o_ref, tok, slot_ref[tok], sem).start()
        _row_copy(h_ref, o_ref, tok, slot_ref[n_tokens + tok], sem).start()
        return 0

    lax.fori_loop(0, tm, issue, 0)

    @pl.when(i == 0)
    def _():
        zero_ref[...] = jnp.zeros_like(zero_ref)
        for e in range(N_EXPERTS):
            lo, hi = pad_ref[e], pad_ref[N_EXPERTS + e]

            def zissue(r, _):
                pltpu.make_async_copy(zero_ref, o_ref.at[pl.ds(r * SUBLANES, SUBLANES), :], zsem).start()
                return 0

            def zwait(r, _):
                pltpu.make_async_copy(zero_ref, o_ref.at[pl.ds(r * SUBLANES, SUBLANES), :], zsem).wait()
                return 0

            lax.fori_loop(lo, hi, zissue, 0)
            lax.fori_loop(lo, hi, zwait, 0)

    def drain(t, _):
        _row_copy(h_ref, o_ref, 0, 0, sem).wait()
        _row_copy(h_ref, o_ref, 0, 0, sem).wait()
        return 0

    lax.fori_loop(0, tm, drain, 0)


def _dispatch(slots, pads, h_tiles, *, capacity, tm, n_tokens):
    seg_rows = h_tiles.shape[0] // n_tokens
    assert seg_rows == SUBLANES
    return pl.pallas_call(
        functools.partial(_dispatch_kernel, tm=tm, n_tokens=n_tokens),
        out_shape=jax.ShapeDtypeStruct((capacity * SUBLANES, LANES), F32),
        grid_spec=pltpu.PrefetchScalarGridSpec(
            num_scalar_prefetch=2,
            grid=(n_tokens // tm,),
            in_specs=[pl.BlockSpec(memory_space=pl.ANY)],
            out_specs=pl.BlockSpec(memory_space=pl.ANY),
            scratch_shapes=[pltpu.VMEM((SUBLANES, LANES), F32),
                            pltpu.SemaphoreType.DMA, pltpu.SemaphoreType.DMA],
        ),
        compiler_params=_cparams(("arbitrary",)),
        name="expert_dispatch",
    )(slots, pads, h_tiles)


def _expert_kernel(te_ref, na_ref, x_ref, w1_ref, w3_ref, w2_ref, o_ref, xb_ref, acc_ref, *, te_rows, n_seg):
    i = pl.program_id(0)
    j = pl.program_id(1)

    @pl.when(i < na_ref[0])
    def _():
        @pl.when(j == 0)
        def _():
            xb_ref[...] = _from_token_tiles(x_ref, te_rows, n_seg).astype(BF16)

        x = xb_ref[...]
        mid = (_silu(_dot(x, w1_ref[...])) * _dot(x, w3_ref[...])).astype(BF16)
        part = _dot(mid, w2_ref[...])

        @pl.when(j == 0)
        def _():
            acc_ref[...] = part

        @pl.when(j == 1)
        def _():
            _to_token_tiles(o_ref, acc_ref[...] + part, te_rows)

    @pl.when(i >= na_ref[0])
    def _():
        o_ref[...] = jnp.zeros_like(o_ref)


def _experts(tile_expert, n_active, xs, w1, w3, w2, *, d):
    n_seg = d // LANES
    te_rows = EXPERT_TILE
    n_tiles = xs.shape[0] // (te_rows * n_seg)
    ff = w1.shape[2]
    fh = ff // 2

    def tile(i, na):
        return jnp.minimum(i, na[0] - 1)

    def half(i, j):
        return jnp.where(i % 2 == 0, j, 1 - j)

    x_spec = pl.BlockSpec((te_rows * n_seg, LANES), lambda i, j, te, na: (tile(i, na), 0))
    y_spec = pl.BlockSpec((te_rows * n_seg, LANES), lambda i, j, te, na: (i, 0))
    return pl.pallas_call(
        functools.partial(_expert_kernel, te_rows=te_rows, n_seg=n_seg),
        out_shape=jax.ShapeDtypeStruct(xs.shape, F32),
        grid_spec=pltpu.PrefetchScalarGridSpec(
            num_scalar_prefetch=2,
            grid=(n_tiles, 2),
            in_specs=[
                x_spec,
                pl.BlockSpec((None, d, fh), lambda i, j, te, na: (te[tile(i, na)], 0, half(i, j))),
                pl.BlockSpec((None, d, fh), lambda i, j, te, na: (te[tile(i, na)], 0, half(i, j))),
                pl.BlockSpec((None, fh, d), lambda i, j, te, na: (te[tile(i, na)], half(i, j), 0)),
            ],
            out_specs=y_spec,
            scratch_shapes=[pltpu.VMEM((te_rows, d), BF16), pltpu.VMEM((te_rows, d), F32)],
        ),
        compiler_params=_cparams(("arbitrary", "arbitrary")),
        name="expert_ffn",
    )(tile_expert, n_active, xs, w1, w3, w2)


def _combine_kernel(slot_ref, x_ref, y_ref, route_ref, mod_ref, g_ref, o_ref, b0_ref, b1_ref, sem, *,
                    tm, tiles_per_batch, n_tokens, n_seg):
    i = pl.program_id(0)
    base = i * tm

    def issue(t, _):
        tok = base + t
        _row_copy(y_ref, b0_ref, slot_ref[tok], t, sem).start()
        _row_copy(y_ref, b1_ref, slot_ref[n_tokens + tok], t, sem).start()
        return 0

    lax.fori_loop(0, tm, issue, 0)

    def drain(t, _):
        _row_copy(y_ref, b0_ref, 0, 0, sem).wait()
        _row_copy(y_ref, b1_ref, 0, 0, sem).wait()
        return 0

    lax.fori_loop(0, tm, drain, 0)

    is_ctx = _is_ctx_rows(tm, tiles_per_batch)
    r = route_ref[...]
    y = r[:, 2:3] * _from_token_tiles(b0_ref, tm, n_seg) + r[:, 3:4] * _from_token_tiles(b1_ref, tm, n_seg)
    o_ref[...] = x_ref[...] + _mod_row(mod_ref, 5, is_ctx) * _rms(y, g_ref[...])


def _combine(slots, x, ys, route, mod, g3, *, tm, tiles_per_batch):
    n, d = x.shape
    n_seg = d // LANES
    kern = functools.partial(_combine_kernel, tm=tm, tiles_per_batch=tiles_per_batch, n_tokens=n, n_seg=n_seg)
    return pl.pallas_call(
        kern,
        out_shape=jax.ShapeDtypeStruct((n, d), F32),
        grid_spec=pltpu.PrefetchScalarGridSpec(
            num_scalar_prefetch=1,
            grid=(n // tm,),
            in_specs=[
                pl.BlockSpec((tm, d), lambda i, s: (i, 0)),
                pl.BlockSpec(memory_space=pl.ANY),
                pl.BlockSpec((tm, LANES), lambda i, s: (i, 0)),
                pl.BlockSpec((None, 2 * SUBLANES, d), lambda i, s: (i // tiles_per_batch, 0, 0)),
                pl.BlockSpec((1, d), lambda i, s: (0, 0)),
            ],
            out_specs=pl.BlockSpec((tm, d), lambda i, s: (i, 0)),
            scratch_shapes=[pltpu.VMEM((tm * n_seg, LANES), F32), pltpu.VMEM((tm * n_seg, LANES), F32),
                            pltpu.SemaphoreType.DMA],
        ),
        compiler_params=_cparams(("arbitrary",)),
        name="expert_combine",
    )(slots, x, ys, route, mod, g3)


def _moe(x1, h_tiles, route, w1, w3, w2, mod, g3, *, tm, tiles_per_batch):
    n, d = x1.shape
    rank, counts = _rank(route, tm=tm)
    counts = counts[0, :N_EXPERTS].astype(jnp.int32)
    padded = ((counts + EXPERT_TILE - 1) // EXPERT_TILE) * EXPERT_TILE
    ends = jnp.cumsum(padded)
    offs = ends - padded
    e12 = route[:, 0:2].astype(jnp.int32)
    slot12 = offs[e12] + rank[:, 0:2].astype(jnp.int32)
    slots = jnp.concatenate([slot12[:, 0], slot12[:, 1]])
    capacity = 2 * n + N_EXPERTS * EXPERT_TILE
    pads = jnp.concatenate([offs + counts, ends.at[-1].set(capacity)]).astype(jnp.int32)
    n_tiles = capacity // EXPERT_TILE
    tile_start = jnp.arange(n_tiles, dtype=jnp.int32) * EXPERT_TILE
    tile_expert = jnp.minimum(jnp.sum(tile_start[:, None] >= ends[None, :], axis=1), N_EXPERTS - 1).astype(jnp.int32)
    n_active = (ends[-1:] // EXPERT_TILE).astype(jnp.int32)

    xs = _dispatch(slots, pads, h_tiles, capacity=capacity, tm=tm, n_tokens=n)
    ys = _experts(tile_expert, n_active, xs, w1, w3, w2, d=d)
    return _combine(slots, x1, ys, route, mod, g3, tm=tm, tiles_per_batch=tiles_per_batch)


def _rope_tables(s_len, dim, lane_lo, block):
    t = jnp.arange(s_len, dtype=jnp.int32)
    row, col = t // GRID_W, t % GRID_W
    n_freq = dim // 4
    inv = 1.0 / (ROPE_BASE ** (jnp.arange(n_freq, dtype=F32) / n_freq))
    ang = jnp.concatenate([row.astype(F32)[:, None] * inv, col.astype(F32)[:, None] * inv], axis=-1)
    cos, sin = jnp.cos(ang), jnp.sin(ang)
    lane = jnp.arange(LANES)
    rel = (lane - lane_lo) % block
    in_rope = (lane >= lane_lo) & (rel < dim)
    idx = rel % (dim // 2)
    first = rel < dim // 2
    c = jnp.where(in_rope[None], cos[:, idx], 1.0)
    s1 = jnp.where((in_rope & first)[None], -sin[:, idx], 0.0)
    s2 = jnp.where((in_rope & ~first)[None], sin[:, idx], 0.0)
    pad = lambda a, v: jnp.concatenate([jnp.full((CTX_LEN, LANES), v, F32), a], axis=0)
    return pad(c, 1.0), pad(s1, 0.0), pad(s2, 0.0)


def _mla_weights(w_uq, w_ukv):
    dq = D_NOPE + D_ROPE
    wq = jnp.zeros((Q_LORA, D_HEADS, LANES), F32).at[:, :, :dq].set(w_uq.reshape(Q_LORA, D_HEADS, dq))
    kv = w_ukv.reshape(KV_LORA, D_HEADS, D_NOPE + D_VDIM)
    wk = jnp.zeros((2 * KV_LORA, D_HEADS, LANES), F32)
    wk = wk.at[:KV_LORA, :, :D_NOPE].set(kv[:, :, :D_NOPE])
    place = jnp.broadcast_to(jnp.eye(D_ROPE, dtype=F32)[:, None, :], (D_ROPE, D_HEADS, D_ROPE))
    wk = wk.at[KV_LORA:KV_LORA + D_ROPE, :, D_NOPE:dq].set(place)
    wv = kv[:, :, D_NOPE:]
    hw = D_HEADS * LANES
    return (wq.reshape(Q_LORA, hw).astype(BF16), wk.reshape(2 * KV_LORA, hw).astype(BF16),
            wv.reshape(KV_LORA, hw).astype(BF16))


def kernel(x, c, ctx, c_ctx, w_mod, b_mod, norm_g, ev_w_in, ev_w_out, sgu_w, sgu_b, sgu_g, diff_lq1, diff_lk1, diff_lq2, diff_lk2, diff_subln_g, ffn_w1, ffn_w3, ffn_w2, od_w_in, od_w_out, na_rpb, mla_gq, mla_w_uq, mla_gkv, mla_w_ukv, router_w, moe_w1, moe_w3, moe_w2):
    bsz, s_len, d = x.shape
    depth = w_mod.shape[0]
    assert ctx.shape[1] == CTX_LEN and bsz + 1 <= SUBLANES and s_len % (NA_GROUP_ROWS * GRID_W) == 0
    t = CTX_LEN + s_len
    n = bsz * t
    tm = 768 if t % 768 == 0 else 256
    tiles_per_batch = t // tm
    grid_rows = s_len // GRID_W

    cond = jnp.zeros((SUBLANES, d), F32).at[:bsz].set(c).at[bsz].set(c_ctx)
    mod = _modulation(cond, w_mod, b_mod).reshape(depth, SUBLANES, N_MOD, d)
    pad2 = jnp.zeros((depth, bsz, SUBLANES - N_MOD, d), F32)
    mod = jnp.concatenate(
        [mod[:, :bsz], pad2, jnp.broadcast_to(mod[:, bsz:bsz + 1], (depth, bsz, N_MOD, d)), pad2], axis=2)

    rope_b = _rope_tables(s_len, B_HEAD_DIM, 0, B_HEAD_DIM)
    rope_d = _rope_tables(s_len, D_ROPE, D_NOPE, LANES)
    no_lam = jnp.zeros((1,), F32)
    no_gain = jnp.ones((1, LANES), F32)

    xs = jnp.concatenate([ctx, x], axis=1).reshape(n, d)
    kw = dict(tm=tm, tiles_per_batch=tiles_per_batch)
    for l in range(depth):
        i = l // 2
        g = norm_g[l].reshape(4, 1, d)
        if l % 2 == 0:
            lam_init = 0.8 - 0.6 * math.exp(-0.3 * l)
            lam = (jnp.exp(jnp.sum(diff_lq1[i] * diff_lk1[i])) - jnp.exp(jnp.sum(diff_lq2[i] * diff_lk2[i]))
                   + lam_init).reshape(1).astype(F32)
            p = _pre_even(xs, mod[l], g[0], ev_w_in[i].astype(BF16), sgu_w[i].astype(BF16), sgu_b[i].T,
                          sgu_g[i].reshape(1, A_WIDTH), rope_b, **kw)
            nb = A_WIDTH // LANES
            o = _attention(p.reshape(bsz, t, -1), lam, diff_subln_g[i].reshape(1, LANES), n_heads=B_HEADS,
                           n_maps=2, qcol=nb, kcol=nb + B_HEADS, vcol=nb + 2 * B_HEADS,
                           post_norm=True, out_scale=1.0 - lam_init)
            x1, h2 = _post(xs, p, o.reshape(n, -1), ev_w_out[i].astype(BF16), mod[l], g[1], g[2], a_col=0, **kw)
            xs = _ffn(x1, h2, ffn_w1[i].astype(BF16), ffn_w3[i].astype(BF16), ffn_w2[i].astype(BF16),
                      mod[l], g[3], **kw)
        else:
            w_in = jnp.pad(od_w_in[i], ((0, 0), (0, LANES - D_ROPE))).astype(BF16)
            wuq, wk, wv = _mla_weights(mla_w_uq[i], mla_w_ukv[i])
            p = _pre_odd(xs, mod[l], g[0], w_in, mla_gq[i].reshape(1, Q_LORA), mla_gkv[i].reshape(1, KV_LORA),
                         wuq, wk, wv, rope_d, **kw)
            p3 = p.reshape(bsz, t, -1)
            o_c = _neighbourhood_attention(p3, _na_bias_tables(na_rpb[i], grid_rows))
            nb = 3 * C_WIDTH // LANES
            o_d = _attention(p3, no_lam, no_gain, n_heads=D_HEADS, n_maps=1, qcol=nb, kcol=nb + D_HEADS,
                             vcol=nb + 2 * D_HEADS, post_norm=False, out_scale=1.0)
            router = jnp.pad(router_w[i], ((0, 0), (0, LANES - N_EXPERTS))).astype(BF16)
            x1, h_tiles, route = _post(xs, o_c.reshape(n, -1), o_d.reshape(n, -1), od_w_out[i].astype(BF16),
                                       mod[l], g[1], g[2], router, a_col=0, **kw)
            xs = _moe(x1, h_tiles, route, moe_w1[i].astype(BF16), moe_w3[i].astype(BF16),
                      moe_w2[i].astype(BF16), mod[l], g[3], **kw)
    return xs.reshape(bsz, t, d)[:, CTX_LEN:]
```

```python
import functools
import math

import jax
import jax.numpy as jnp
from jax import lax
from jax.experimental import pallas as pl
from jax.experimental.pallas import tpu as pltpu

F32 = jnp.float32
BF16 = jnp.bfloat16

GRID_W = 64
CTX_LEN = 256
N_MOD = 6
NORM_EPS = 1e-6
ROPE_BASE = 10000.0
NEG_INF = -1e30
A_GROUPS = 4
A_WIDTH = 512
CHUNK = 128
B_HEADS = 4
B_HEAD_DIM = 64
B_WIDTH = 512
C_HEADS = 8
C_HEAD_DIM = 64
C_WIDTH = 512
WIN_H = 8
WIN_W = 16
D_HEADS = 4
D_NOPE = 64
D_ROPE = 32
D_VDIM = 128
Q_LORA = 256
KV_LORA = 128
N_EXPERTS = 8
LOG2E = 1.4426950408889634

LANES = 128
SUBLANES = 8
VMEM_LIMIT = 56 * 1024 * 1024

NA_GROUP_ROWS = 4
NA_WIN_ROWS = WIN_H + NA_GROUP_ROWS - 1
EXPERT_TILE = 512


def _cparams(sem):
    return pltpu.CompilerParams(dimension_semantics=sem, vmem_limit_bytes=VMEM_LIMIT)


def _dot(a, b):
    return jnp.dot(a, b, preferred_element_type=F32)


def _dot_nt(a, b):
    return lax.dot_general(a, b, (((1,), (1,)), ((), ())), preferred_element_type=F32)


def _rms(x, g):
    return x * lax.rsqrt(jnp.mean(x * x, axis=-1, keepdims=True) + NORM_EPS) * g


def _gelu_tanh(x):
    return 0.5 * x * (1.0 + jnp.tanh(math.sqrt(2.0 / math.pi) * (x + 0.044715 * (x * x * x))))


def _silu(x):
    return x / (1.0 + jnp.exp(-x))


def _is_ctx_rows(tm, tiles_per_batch):
    i = pl.program_id(0)
    row = (i % tiles_per_batch) * tm + lax.broadcasted_iota(jnp.int32, (tm, 1), 0)
    return row < CTX_LEN


def _mod_row(mod_ref, k, is_ctx):
    return jnp.where(is_ctx, mod_ref[SUBLANES + k:SUBLANES + k + 1, :], mod_ref[k:k + 1, :])


def _rope_block(xb, c, s1, s2, shift):
    return xb * c + pltpu.roll(xb, LANES - shift, 1) * s1 + pltpu.roll(xb, shift, 1) * s2


def _to_token_tiles(dst_ref, val, rows):
    for s in range(val.shape[1] // LANES):
        dst_ref[pl.ds(s, rows, stride=SUBLANES), :] = val[:, s * LANES:(s + 1) * LANES]


def _from_token_tiles(src_ref, rows, n_seg):
    return jnp.concatenate(
        [src_ref[pl.ds(s, rows, stride=SUBLANES), :] for s in range(n_seg)], axis=1)


def _mod_kernel(s_ref, w_ref, b_ref, o_ref):
    s = _silu(s_ref[...])
    o_ref[...] = _dot(s.astype(BF16), w_ref[...].astype(BF16)) + b_ref[...]


def _modulation(cond, w_mod, b_mod):
    depth, d, _ = w_mod.shape
    return pl.pallas_call(
        _mod_kernel,
        out_shape=jax.ShapeDtypeStruct((depth, SUBLANES, N_MOD * d), F32),
        grid=(depth, N_MOD),
        in_specs=[
            pl.BlockSpec((SUBLANES, d), lambda l, j: (0, 0)),
            pl.BlockSpec((None, d, d), lambda l, j: (l, 0, j)),
            pl.BlockSpec((None, 1, d), lambda l, j: (l, 0, j)),
        ],
        out_specs=pl.BlockSpec((None, SUBLANES, d), lambda l, j: (l, 0, j)),
        compiler_params=_cparams(("parallel", "parallel")),
        name="modulation",
    )(cond, w_mod, b_mod.reshape(depth, 1, N_MOD * d))


def _pre_even_kernel(x_ref, mod_ref, g_ref, w_ref, sw_ref, sb_ref, sg_ref, c_ref, s1_ref, s2_ref,
                     o_ref, *, tm, tiles_per_batch, qscale):
    is_ctx = _is_ctx_rows(tm, tiles_per_batch)
    n = _rms(x_ref[...], g_ref[...])
    h = (n * (1.0 + _mod_row(mod_ref, 1, is_ctx)) + _mod_row(mod_ref, 0, is_ctx)).astype(BF16)

    u = _gelu_tanh(_dot(h, w_ref[:, 0:A_WIDTH]))
    v = _gelu_tanh(_dot(h, w_ref[:, A_WIDTH:2 * A_WIDTH]))
    vc = v - jnp.mean(v, axis=-1, keepdims=True)
    vn = (vc * lax.rsqrt(jnp.mean(vc * vc, axis=-1, keepdims=True) + NORM_EPS) * sg_ref[...]).astype(BF16)
    gd = A_WIDTH // A_GROUPS
    for c in range(tm // CHUNK):
        rs = slice(c * CHUNK, (c + 1) * CHUNK)
        for g in range(A_GROUPS):
            cs = slice(g * gd, (g + 1) * gd)
            mixed = _dot(sw_ref[g], vn[rs, cs]) + sb_ref[:, g:g + 1]
            o_ref[rs, cs] = (u[rs, cs] * mixed).astype(BF16)

    c, s1, s2 = c_ref[...], s1_ref[...], s2_ref[...]
    q = _dot(h, w_ref[:, 2 * A_WIDTH:2 * A_WIDTH + B_WIDTH])
    k = _dot(h, w_ref[:, 2 * A_WIDTH + B_WIDTH:2 * A_WIDTH + 2 * B_WIDTH])
    for j in range(B_WIDTH // LANES):
        ls = slice(j * LANES, (j + 1) * LANES)
        o_ref[:, A_WIDTH + j * LANES:A_WIDTH + (j + 1) * LANES] = (
            _rope_block(q[:, ls], c, s1, s2, B_HEAD_DIM // 2) * qscale).astype(BF16)
        o_ref[:, A_WIDTH + B_WIDTH + j * LANES:A_WIDTH + B_WIDTH + (j + 1) * LANES] = (
            _rope_block(k[:, ls], c, s1, s2, B_HEAD_DIM // 2)).astype(BF16)
    o_ref[:, A_WIDTH + 2 * B_WIDTH:] = _dot(h, w_ref[:, 2 * A_WIDTH + 2 * B_WIDTH:]).astype(BF16)


def _pre_even(x, mod, g0, w_in, sgu_w, sgu_bt, sgu_g, rope, *, tm, tiles_per_batch):
    n, d = x.shape
    t = tiles_per_batch * tm
    wout = A_WIDTH + 3 * B_WIDTH
    const = lambda i: (0, 0)
    kern = functools.partial(_pre_even_kernel, tm=tm, tiles_per_batch=tiles_per_batch,
                             qscale=LOG2E / math.sqrt(B_HEAD_DIM))
    rope_spec = pl.BlockSpec((tm, LANES), lambda i: (i % tiles_per_batch, 0))
    return pl.pallas_call(
        kern,
        out_shape=jax.ShapeDtypeStruct((n, wout), BF16),
        grid=(n // tm,),
        in_specs=[
            pl.BlockSpec((tm, d), lambda i: (i, 0)),
            pl.BlockSpec((None, 2 * SUBLANES, d), lambda i: (i // tiles_per_batch, 0, 0)),
            pl.BlockSpec((1, d), const),
            pl.BlockSpec(w_in.shape, const),
            pl.BlockSpec(sgu_w.shape, lambda i: (0, 0, 0)),
            pl.BlockSpec(sgu_bt.shape, const),
            pl.BlockSpec(sgu_g.shape, const),
            rope_spec, rope_spec, rope_spec,
        ],
        out_specs=pl.BlockSpec((tm, wout), lambda i: (i, 0)),
        compiler_params=_cparams(("parallel",)),
        name="pre_even",
    )(x, mod, g0, w_in, sgu_w, sgu_bt, sgu_g, *rope)


def _pre_odd_kernel(x_ref, mod_ref, g_ref, w_ref, gq_ref, gkv_ref, wuq_ref, wk_ref, wv_ref,
                    c_ref, s1_ref, s2_ref, o_ref, *, tm, tiles_per_batch, scale_c, scale_d):
    is_ctx = _is_ctx_rows(tm, tiles_per_batch)
    n = _rms(x_ref[...], g_ref[...])
    h = (n * (1.0 + _mod_row(mod_ref, 1, is_ctx)) + _mod_row(mod_ref, 0, is_ctx)).astype(BF16)

    o_ref[:, 0:C_WIDTH] = (_dot(h, w_ref[:, 0:C_WIDTH]) * scale_c).astype(BF16)
    o_ref[:, C_WIDTH:3 * C_WIDTH] = _dot(h, w_ref[:, C_WIDTH:3 * C_WIDTH]).astype(BF16)

    low = _dot(h, w_ref[:, 3 * C_WIDTH:])
    cq = _rms(low[:, 0:Q_LORA], gq_ref[...]).astype(BF16)
    ckv = _rms(low[:, Q_LORA:Q_LORA + KV_LORA], gkv_ref[...]).astype(BF16)
    kr = low[:, Q_LORA + KV_LORA:].astype(BF16)
    qd = _dot(cq, wuq_ref[...])
    kd = _dot(jnp.concatenate([ckv, kr], axis=1), wk_ref[...])
    vd = _dot(ckv, wv_ref[...])
    c, s1, s2 = c_ref[...], s1_ref[...], s2_ref[...]
    base = 3 * C_WIDTH
    hw = D_HEADS * LANES
    for j in range(D_HEADS):
        ls = slice(j * LANES, (j + 1) * LANES)
        o_ref[:, base + j * LANES:base + (j + 1) * LANES] = (
            _rope_block(qd[:, ls], c, s1, s2, D_ROPE // 2) * scale_d).astype(BF16)
        o_ref[:, base + hw + j * LANES:base + hw + (j + 1) * LANES] = (
            _rope_block(kd[:, ls], c, s1, s2, D_ROPE // 2)).astype(BF16)
    o_ref[:, base + 2 * hw:] = vd.astype(BF16)


def _pre_odd(x, mod, g0, w_in, gq, gkv, wuq, wk, wv, rope, *, tm, tiles_per_batch):
    n, d = x.shape
    wout = 3 * C_WIDTH + 3 * D_HEADS * LANES
    const = lambda i: (0, 0)
    kern = functools.partial(_pre_odd_kernel, tm=tm, tiles_per_batch=tiles_per_batch,
                             scale_c=LOG2E / math.sqrt(C_HEAD_DIM),
                             scale_d=LOG2E / math.sqrt(D_NOPE + D_ROPE))
    rope_spec = pl.BlockSpec((tm, LANES), lambda i: (i % tiles_per_batch, 0))
    return pl.pallas_call(
        kern,
        out_shape=jax.ShapeDtypeStruct((n, wout), BF16),
        grid=(n // tm,),
        in_specs=[
            pl.BlockSpec((tm, d), lambda i: (i, 0)),
            pl.BlockSpec((None, 2 * SUBLANES, d), lambda i: (i // tiles_per_batch, 0, 0)),
            pl.BlockSpec((1, d), const),
            pl.BlockSpec(w_in.shape, const),
            pl.BlockSpec(gq.shape, const),
            pl.BlockSpec(gkv.shape, const),
            pl.BlockSpec(wuq.shape, const),
            pl.BlockSpec(wk.shape, const),
            pl.BlockSpec(wv.shape, const),
            rope_spec, rope_spec, rope_spec,
        ],
        out_specs=pl.BlockSpec((tm, wout), lambda i: (i, 0)),
        compiler_params=_cparams(("parallel",)),
        name="pre_odd",
    )(x, mod, g0, w_in, gq, gkv, wuq, wk, wv, *rope)


def _attn_kernel(lam_ref, q_ref, k_ref, v_ref, g_ref, o_ref, vt_ref, s_ref, *, n_maps, tq, tk, n_chunks,
                 post_norm, out_scale):
    qi = pl.program_id(2)

    @pl.when(qi == 0)
    def _():
        eye = (lax.broadcasted_iota(jnp.int32, (LANES, LANES), 0)
               == lax.broadcasted_iota(jnp.int32, (LANES, LANES), 1)).astype(BF16)
        for c in range(n_chunks):
            vt_ref[c] = _dot_nt(eye, v_ref[c * tk:(c + 1) * tk, :]).astype(BF16)

    q = q_ref[...]
    if n_maps == 2:
        lane = lax.broadcasted_iota(jnp.int32, (1, LANES), 1)
        half = LANES // 2
        qs = [jnp.where(lane < half, q, jnp.zeros_like(q)), jnp.where(lane >= half, q, jnp.zeros_like(q))]
    else:
        qs = [q]

    def fold(x, op):
        return op(x.reshape(x.shape[0] // SUBLANES, SUBLANES, tq), axis=0)

    def scores(start, rows, mx):
        ks = k_ref[pl.ds(start, rows), :]
        new = []
        for a, qm in enumerate(qs):
            st = _dot_nt(ks, qm)
            s_ref[a, pl.ds(start, rows), :] = st
            new.append(jnp.maximum(mx[a], fold(st, jnp.max)))
        return tuple(new)

    def weigh(start, rows, vt, m, carry):
        new = []
        for a in range(n_maps):
            l, acc = carry[a]
            p = jnp.exp2(s_ref[a, pl.ds(start, rows), :] - m[a])
            new.append((l + fold(p, jnp.sum), acc + _dot(vt, p.astype(BF16))))
        return tuple(new)

    mx0 = tuple(jnp.full((SUBLANES, tq), NEG_INF, F32) for _ in qs)
    acc0 = tuple((jnp.zeros((SUBLANES, tq), F32), jnp.zeros((LANES, tq), F32)) for _ in qs)

    def col_max(mx):
        return tuple(jnp.max(x, axis=0, keepdims=True) for x in mx)

    def finish(carry):
        inv = [1.0 / jnp.sum(l, axis=0, keepdims=True) for l, _ in carry]
        ot = carry[0][1] * inv[0]
        if n_maps == 2:
            ot = ot - lam_ref[0] * (carry[1][1] * inv[1])
        o = ot.T
        if post_norm:
            o = _rms(o, g_ref[...]) * out_scale
        o_ref[...] = o.astype(o_ref.dtype)

    @pl.when(qi == 0)
    def _():
        m = col_max(scores(0, CTX_LEN, mx0))
        finish(weigh(0, CTX_LEN, vt_ref[0, :, 0:CTX_LEN], m, acc0))

    @pl.when(qi > 0)
    def _():
        def pass1(c, mx):
            return scores(pl.multiple_of(c * tk, tk), tk, mx)

        m = col_max(lax.fori_loop(0, n_chunks, pass1, mx0))

        def pass2(c, carry):
            return weigh(pl.multiple_of(c * tk, tk), tk, vt_ref[c], m, carry)

        finish(lax.fori_loop(0, n_chunks, pass2, acc0))


def _key_chunk(t):
    for tk in (2816, 1280, 768, 512, 256):
        if t % tk == 0:
            return tk
    raise ValueError(f"unsupported sequence length {t}")


def _attention(p3, lam, gain, *, n_heads, n_maps, qcol, kcol, vcol, post_norm, out_scale):
    b, t, _ = p3.shape
    tq = CTX_LEN
    tk = _key_chunk(t)
    kern = functools.partial(_attn_kernel, n_maps=n_maps, tq=tq, tk=tk, n_chunks=t // tk,
                             post_norm=post_norm, out_scale=out_scale)
    return pl.pallas_call(
        kern,
        out_shape=jax.ShapeDtypeStruct((b, t, n_heads * LANES), BF16),
        grid=(b, n_heads, t // tq),
        in_specs=[
            pl.BlockSpec(memory_space=pltpu.SMEM),
            pl.BlockSpec((None, tq, LANES), lambda bi, h, qi: (bi, qi, qcol + h)),
            pl.BlockSpec((None, t, LANES), lambda bi, h, qi: (bi, 0, kcol + h)),
            pl.BlockSpec((None, t, LANES), lambda bi, h, qi: (bi, 0, vcol + h)),
            pl.BlockSpec((1, LANES), lambda bi, h, qi: (0, 0)),
        ],
        out_specs=pl.BlockSpec((None, tq, LANES), lambda bi, h, qi: (bi, qi, h)),
        scratch_shapes=[pltpu.VMEM((t // tk, LANES, tk), BF16), pltpu.VMEM((n_maps, t, tq), F32)],
        compiler_params=_cparams(("parallel", "parallel", "arbitrary")),
        name="attention_maps%d" % n_maps,
    )(lam, p3, p3, p3, gain)


def _na_kernel(q_ref, k_ref, v_ref, bias_ref, o_ref, *, tq, n_groups, grid_rows):
    qi = pl.program_id(2)
    q = q_ref[...]
    lane = lax.broadcasted_iota(jnp.int32, (1, LANES), 1)
    half = LANES // 2
    kctx = k_ref[0:CTX_LEN, :]
    vctx = v_ref[0:CTX_LEN, :]
    win = NA_WIN_ROWS * GRID_W

    def head_out(hl, local):
        qm = jnp.where((lane < half) if hl == 0 else (lane >= half), q, jnp.zeros_like(q))
        s_c = _dot_nt(qm, kctx)
        m = jnp.max(s_c, axis=-1, keepdims=True)
        if local is not None:
            kloc, vloc, tau = local
            s_l = _dot_nt(qm, kloc) + bias_ref[tau, hl]
            m = jnp.maximum(m, jnp.max(s_l, axis=-1, keepdims=True))
        p_c = jnp.exp2(s_c - m)
        l = jnp.sum(p_c, axis=-1, keepdims=True)
        o = _dot(p_c.astype(BF16), vctx)
        if local is not None:
            p_l = jnp.exp2(s_l - m)
            l = l + jnp.sum(p_l, axis=-1, keepdims=True)
            o = o + _dot(p_l.astype(BF16), vloc)
        return o * (1.0 / l)

    def write(local):
        o_ref[...] = jnp.where(lane < half, head_out(0, local), head_out(1, local)).astype(o_ref.dtype)

    @pl.when(qi == 0)
    def _():
        write(None)

    @pl.when(qi > 0)
    def _():
        g = qi - 1
        krow = jnp.clip(NA_GROUP_ROWS * g - WIN_H // 2, 0, grid_rows - NA_WIN_ROWS)
        start = pl.multiple_of(CTX_LEN + GRID_W * krow, GRID_W)
        tau = jnp.where(g == 0, 0, jnp.where(g == n_groups - 1, 2, 1))
        write((k_ref[pl.ds(start, win), :], v_ref[pl.ds(start, win), :], tau))


def _na_bias_tables(rpb, grid_rows):
    n_groups = grid_rows // NA_GROUP_ROWS
    cq = jnp.arange(GRID_W)
    cs = jnp.clip(cq - WIN_W // 2, 0, GRID_W - WIN_W)
    col_ok = (cq[None, :] >= cs[:, None]) & (cq[None, :] < cs[:, None] + WIN_W)
    co = jnp.clip(cq[None, :] - cq[:, None] + (WIN_W - 1), 0, 2 * WIN_W - 2)
    co_sel = (co[:, :, None] == jnp.arange(2 * WIN_W - 1)).astype(F32)
    tabs = []
    for g in (0, 1, n_groups - 1):
        r0 = NA_GROUP_ROWS * g
        krow0 = min(max(r0 - WIN_H // 2, 0), grid_rows - NA_WIN_ROWS)
        r = r0 + jnp.arange(NA_GROUP_ROWS)
        rs = jnp.clip(r - WIN_H // 2, 0, grid_rows - WIN_H)
        kr = krow0 + jnp.arange(NA_WIN_ROWS)
        row_ok = (kr[None, :] >= rs[:, None]) & (kr[None, :] < rs[:, None] + WIN_H)
        ro = jnp.clip(kr[None, :] - r[:, None] + (WIN_H - 1), 0, 2 * WIN_H - 2)
        ro_sel = (ro[:, :, None] == jnp.arange(2 * WIN_H - 1)).astype(F32)
        bias = jnp.einsum("hrc,ijr,qkc->hiqjk", rpb.astype(F32), ro_sel, co_sel,
                          precision=lax.Precision.HIGHEST)
        ok = row_ok[:, None, :, None] & col_ok[None, :, None, :]
        bias = jnp.where(ok[None], bias.astype(F32) * LOG2E, NEG_INF)
        tabs.append(bias.reshape(rpb.shape[0], NA_GROUP_ROWS * GRID_W, NA_WIN_ROWS * GRID_W))
    return jnp.stack(tabs)


def _neighbourhood_attention(p3, bias):
    b, t, _ = p3.shape
    tq = NA_GROUP_ROWS * GRID_W
    grid_rows = (t - CTX_LEN) // GRID_W
    n_groups = grid_rows // NA_GROUP_ROWS
    n_pairs = C_WIDTH // LANES
    kern = functools.partial(_na_kernel, tq=tq, n_groups=n_groups, grid_rows=grid_rows)
    return pl.pallas_call(
        kern,
        out_shape=jax.ShapeDtypeStruct((b, t, C_WIDTH), BF16),
        grid=(b, n_pairs, t // tq),
        in_specs=[
            pl.BlockSpec((None, tq, LANES), lambda bi, h, qi: (bi, qi, h)),
            pl.BlockSpec((None, t, LANES), lambda bi, h, qi: (bi, 0, n_pairs + h)),
            pl.BlockSpec((None, t, LANES), lambda bi, h, qi: (bi, 0, 2 * n_pairs + h)),
            pl.BlockSpec((3, 2, tq, NA_WIN_ROWS * GRID_W), lambda bi, h, qi: (0, h, 0, 0)),
        ],
        out_specs=pl.BlockSpec((None, tq, LANES), lambda bi, h, qi: (bi, qi, h)),
        compiler_params=_cparams(("parallel", "parallel", "arbitrary")),
        name="neighbourhood_attention",
    )(p3, p3, p3, bias)


def _post_common(x_ref, a_ref, b_ref, w_ref, mod_ref, g1_ref, g2_ref, is_ctx):
    wa = a_ref.shape[1]
    y = _dot(a_ref[...], w_ref[0:wa, :]) + _dot(b_ref[...], w_ref[wa:, :])
    x1 = x_ref[...] + _mod_row(mod_ref, 2, is_ctx) * _rms(y, g1_ref[...])
    h2 = _rms(x1, g2_ref[...]) * (1.0 + _mod_row(mod_ref, 4, is_ctx)) + _mod_row(mod_ref, 3, is_ctx)
    return x1, h2


def _post_even_kernel(x_ref, a_ref, b_ref, w_ref, mod_ref, g1_ref, g2_ref, x1_ref, h_ref, *,
                      tm, tiles_per_batch):
    is_ctx = _is_ctx_rows(tm, tiles_per_batch)
    x1, h2 = _post_common(x_ref, a_ref, b_ref, w_ref, mod_ref, g1_ref, g2_ref, is_ctx)
    x1_ref[...] = x1
    h_ref[...] = h2.astype(BF16)


def _post_odd_kernel(x_ref, a_ref, b_ref, w_ref, mod_ref, g1_ref, g2_ref, wr_ref,
                     x1_ref, ht_ref, route_ref, *, tm, tiles_per_batch):
    is_ctx = _is_ctx_rows(tm, tiles_per_batch)
    x1, h2 = _post_common(x_ref, a_ref, b_ref, w_ref, mod_ref, g1_ref, g2_ref, is_ctx)
    x1_ref[...] = x1
    _to_token_tiles(ht_ref, h2, tm)

    lane = lax.broadcasted_iota(jnp.int32, (tm, LANES), 1)
    logits = jnp.where(lane < N_EXPERTS, _dot(h2.astype(BF16), wr_ref[...]), NEG_INF)
    m1 = jnp.max(logits, axis=-1, keepdims=True)
    i1 = jnp.min(jnp.where(logits == m1, lane, LANES), axis=-1, keepdims=True)
    rest = jnp.where(lane == i1, NEG_INF, logits)
    m2 = jnp.max(rest, axis=-1, keepdims=True)
    i2 = jnp.min(jnp.where(rest == m2, lane, LANES), axis=-1, keepdims=True)
    e = jnp.exp(m2 - m1)
    gate1 = 1.0 / (1.0 + e)
    gate2 = e * gate1
    route_ref[...] = jnp.where(
        lane == 0, i1.astype(F32),
        jnp.where(lane == 1, i2.astype(F32),
                  jnp.where(lane == 2, gate1, jnp.where(lane == 3, gate2, 0.0))))


def _post(x, a, b, w_out, mod, g1, g2, router=None, *, a_col, tm, tiles_per_batch):
    n, d = x.shape
    wa, wb = w_out.shape[0] - b.shape[1], b.shape[1]
    const = lambda i: (0, 0)
    in_specs = [
        pl.BlockSpec((tm, d), lambda i: (i, 0)),
        pl.BlockSpec((tm, wa), lambda i: (i, a_col)),
        pl.BlockSpec((tm, wb), lambda i: (i, 0)),
        pl.BlockSpec(w_out.shape, const),
        pl.BlockSpec((None, 2 * SUBLANES, d), lambda i: (i // tiles_per_batch, 0, 0)),
        pl.BlockSpec((1, d), const),
        pl.BlockSpec((1, d), const),
    ]
    row_spec = pl.BlockSpec((tm, d), lambda i: (i, 0))
    if router is None:
        kern = functools.partial(_post_even_kernel, tm=tm, tiles_per_batch=tiles_per_batch)
        return pl.pallas_call(
            kern,
            out_shape=(jax.ShapeDtypeStruct((n, d), F32), jax.ShapeDtypeStruct((n, d), BF16)),
            grid=(n // tm,),
            in_specs=in_specs,
            out_specs=(row_spec, row_spec),
            compiler_params=_cparams(("parallel",)),
            name="post_even",
        )(x, a, b, w_out, mod, g1, g2)
    kern = functools.partial(_post_odd_kernel, tm=tm, tiles_per_batch=tiles_per_batch)
    seg = d // LANES
    return pl.pallas_call(
        kern,
        out_shape=(jax.ShapeDtypeStruct((n, d), F32),
                   jax.ShapeDtypeStruct((n * seg, LANES), F32),
                   jax.ShapeDtypeStruct((n, LANES), F32)),
        grid=(n // tm,),
        in_specs=in_specs + [pl.BlockSpec(router.shape, const)],
        out_specs=(row_spec,
                   pl.BlockSpec((tm * seg, LANES), lambda i: (i, 0)),
                   pl.BlockSpec((tm, LANES), lambda i: (i, 0))),
        compiler_params=_cparams(("parallel",)),
        name="post_odd",
    )(x, a, b, w_out, mod, g1, g2, router)


def _ffn_kernel(x_ref, h_ref, w1_ref, w3_ref, w2_ref, mod_ref, g_ref, o_ref, *, tm, tiles_per_batch,
                n_split):
    is_ctx = _is_ctx_rows(tm, tiles_per_batch)
    h = h_ref[...]
    fc = w1_ref.shape[1] // n_split
    y = None
    for j in range(n_split):
        cs = slice(j * fc, (j + 1) * fc)
        mid = (_silu(_dot(h, w1_ref[:, cs])) * _dot(h, w3_ref[:, cs])).astype(BF16)
        part = _dot(mid, w2_ref[cs, :])
        y = part if y is None else y + part
    o_ref[...] = x_ref[...] + _mod_row(mod_ref, 5, is_ctx) * _rms(y, g_ref[...])


def _ffn(x, h, w1, w3, w2, mod, g3, *, tm, tiles_per_batch):
    n, d = x.shape
    const = lambda i: (0, 0)
    ff = w1.shape[1]
    n_split = 2 if ff % (2 * LANES) == 0 else 1
    kern = functools.partial(_ffn_kernel, tm=tm, tiles_per_batch=tiles_per_batch, n_split=n_split)
    row_spec = pl.BlockSpec((tm, d), lambda i: (i, 0))
    return pl.pallas_call(
        kern,
        out_shape=jax.ShapeDtypeStruct((n, d), F32),
        grid=(n // tm,),
        in_specs=[
            row_spec, row_spec,
            pl.BlockSpec(w1.shape, const, pipeline_mode=pl.Buffered(1)),
            pl.BlockSpec(w3.shape, const, pipeline_mode=pl.Buffered(1)),
            pl.BlockSpec(w2.shape, const, pipeline_mode=pl.Buffered(1)),
            pl.BlockSpec((None, 2 * SUBLANES, d), lambda i: (i // tiles_per_batch, 0, 0)),
            pl.BlockSpec((1, d), const),
        ],
        out_specs=row_spec,
        compiler_params=_cparams(("parallel",)),
        name="dense_ffn",
    )(x, h, w1, w3, w2, mod, g3)


def _rank_kernel(route_ref, tri_ref, rank_ref, count_ref, carry_ref, *, tm):
    i = pl.program_id(0)

    @pl.when(i == 0)
    def _():
        carry_ref[...] = jnp.zeros_like(carry_ref)

    lane = lax.broadcasted_iota(jnp.int32, (tm, LANES), 1)
    r = route_ref[...]
    oh1 = (lane == r[:, 0:1].astype(jnp.int32)).astype(F32)
    oh2 = (lane == r[:, 1:2].astype(jnp.int32)).astype(F32)
    both = oh1 + oh2
    before = _dot(tri_ref[...], both.astype(BF16)) + carry_ref[0:1, :]
    rank1 = jnp.sum(oh1 * before, axis=-1, keepdims=True)
    rank2 = jnp.sum(oh2 * before, axis=-1, keepdims=True)
    rank_ref[...] = jnp.where(lane == 0, rank1, jnp.where(lane == 1, rank2, 0.0))
    carry_ref[...] = carry_ref[...] + jnp.sum(both, axis=0, keepdims=True)
    count_ref[...] = carry_ref[...]


def _rank(route, *, tm):
    n = route.shape[0]
    tri = (jnp.arange(tm)[:, None] > jnp.arange(tm)[None, :]).astype(BF16)
    return pl.pallas_call(
        functools.partial(_rank_kernel, tm=tm),
        out_shape=(jax.ShapeDtypeStruct((n, LANES), F32), jax.ShapeDtypeStruct((SUBLANES, LANES), F32)),
        grid=(n // tm,),
        in_specs=[pl.BlockSpec((tm, LANES), lambda i: (i, 0)),
                  pl.BlockSpec((tm, tm), lambda i: (0, 0))],
        out_specs=(pl.BlockSpec((tm, LANES), lambda i: (i, 0)),
                   pl.BlockSpec((SUBLANES, LANES), lambda i: (0, 0))),
        scratch_shapes=[pltpu.VMEM((SUBLANES, LANES), F32)],
        compiler_params=_cparams(("arbitrary",)),
        name="expert_rank",
    )(route, tri)


def _row_copy(src, dst, src_row, dst_row, sem):
    return pltpu.make_async_copy(src.at[pl.ds(src_row * SUBLANES, SUBLANES), :],
                                 dst.at[pl.ds(dst_row * SUBLANES, SUBLANES), :], sem)


def _dispatch_kernel(slot_ref, pad_ref, h_ref, o_ref, zero_ref, sem, zsem, *, tm, n_tokens):
    i = pl.program_id(0)
    base = i * tm

    def issue(t, _):
        tok = base + t
        _row_copy(h_ref, o_ref, t, slot_ref[tok], sem).start()
        _row_copy(h_ref, o_ref, t, slot_ref[n_tokens + tok], sem).start()
        return 0

    lax.fori_loop(0, tm, issue, 0)

    @pl.when(i == 0)
    def _():
        zero_ref[...] = jnp.zeros_like(zero_ref)
        for e in range(N_EXPERTS):
            lo, hi = pad_ref[e], pad_ref[N_EXPERTS + e]

            def zissue(r, _):
                pltpu.make_async_copy(zero_ref, o_ref.at[pl.ds(r * SUBLANES, SUBLANES), :], zsem).start()
                return 0

            def zwait(r, _):
                pltpu.make_async_copy(zero_ref, o_ref.at[pl.ds(r * SUBLANES, SUBLANES), :], zsem).wait()
                return 0

            lax.fori_loop(lo, hi, zissue, 0)
            lax.fori_loop(lo, hi, zwait, 0)

    def drain(t, _):
        _row_copy(h_ref, o_ref, 0, 0, sem).wait()
        _row_copy(h_ref, o_ref, 0, 0, sem).wait()
        return 0

    lax.fori_loop(0, tm, drain, 0)


def _dispatch(slots, pads, h_tiles, *, capacity, tm, n_tokens):
    seg_rows = h_tiles.shape[0] // n_tokens
    assert seg_rows == SUBLANES
    return pl.pallas_call(
        functools.partial(_dispatch_kernel, tm=tm, n_tokens=n_tokens),
        out_shape=jax.ShapeDtypeStruct((capacity * SUBLANES, LANES), F32),
        grid_spec=pltpu.PrefetchScalarGridSpec(
            num_scalar_prefetch=2,
            grid=(n_tokens // tm,),
            in_specs=[pl.BlockSpec((tm * SUBLANES, LANES), lambda i, s, p: (i, 0))],
            out_specs=pl.BlockSpec(memory_space=pl.ANY),
            scratch_shapes=[pltpu.VMEM((SUBLANES, LANES), F32),
                            pltpu.SemaphoreType.DMA, pltpu.SemaphoreType.DMA],
        ),
        compiler_params=_cparams(("arbitrary",)),
        name="expert_dispatch",
    )(slots, pads, h_tiles)


def _expert_kernel(te_ref, na_ref, x_ref, w1_ref, w3_ref, w2_ref, o_ref, xb_ref, acc_ref, *, te_rows, n_seg):
    i = pl.program_id(0)
    j = pl.program_id(1)

    @pl.when(i < na_ref[0])
    def _():
        @pl.when(j == 0)
        def _():
            xb_ref[...] = _from_token_tiles(x_ref, te_rows, n_seg).astype(BF16)

        x = xb_ref[...]
        mid = (_silu(_dot(x, w1_ref[...])) * _dot(x, w3_ref[...])).astype(BF16)
        part = _dot(mid, w2_ref[...])

        @pl.when(j == 0)
        def _():
            acc_ref[...] = part

        @pl.when(j == 1)
        def _():
            _to_token_tiles(o_ref, acc_ref[...] + part, te_rows)

    @pl.when(i >= na_ref[0])
    def _():
        o_ref[...] = jnp.zeros_like(o_ref)


def _experts(tile_expert, n_active, xs, w1, w3, w2, *, d):
    n_seg = d // LANES
    te_rows = EXPERT_TILE
    n_tiles = xs.shape[0] // (te_rows * n_seg)
    ff = w1.shape[2]
    fh = ff // 2

    def tile(i, na):
        return jnp.minimum(i, na[0] - 1)

    def half(i, j):
        return jnp.where(i % 2 == 0, j, 1 - j)

    x_spec = pl.BlockSpec((te_rows * n_seg, LANES), lambda i, j, te, na: (tile(i, na), 0))
    y_spec = pl.BlockSpec((te_rows * n_seg, LANES), lambda i, j, te, na: (i, 0))
    return pl.pallas_call(
        functools.partial(_expert_kernel, te_rows=te_rows, n_seg=n_seg),
        out_shape=jax.ShapeDtypeStruct(xs.shape, F32),
        grid_spec=pltpu.PrefetchScalarGridSpec(
            num_scalar_prefetch=2,
            grid=(n_tiles, 2),
            in_specs=[
                x_spec,
                pl.BlockSpec((None, d, fh), lambda i, j, te, na: (te[tile(i, na)], 0, half(i, j))),
                pl.BlockSpec((None, d, fh), lambda i, j, te, na: (te[tile(i, na)], 0, half(i, j))),
                pl.BlockSpec((None, fh, d), lambda i, j, te, na: (te[tile(i, na)], half(i, j), 0)),
            ],
            out_specs=y_spec,
            scratch_shapes=[pltpu.VMEM((te_rows, d), BF16), pltpu.VMEM((te_rows, d), F32)],
        ),
        compiler_params=_cparams(("arbitrary", "arbitrary")),
        name="expert_ffn",
    )(tile_expert, n_active, xs, w1, w3, w2)


def _combine_kernel(slot_ref, x_ref, y_ref, route_ref, mod_ref, g_ref, o_ref, b0_ref, b1_ref, sem, *,
                    tm, tiles_per_batch, n_tokens, n_seg):
    i = pl.program_id(0)
    base = i * tm

    def issue(t, _):
        tok = base + t
        _row_copy(y_ref, b0_ref, slot_ref[tok], t, sem).start()
        _row_copy(y_ref, b1_ref, slot_ref[n_tokens + tok], t, sem).start()
        return 0

    lax.fori_loop(0, tm, issue, 0)

    def drain(t, _):
        _row_copy(y_ref, b0_ref, 0, 0, sem).wait()
        _row_copy(y_ref, b1_ref, 0, 0, sem).wait()
        return 0

    lax.fori_loop(0, tm, drain, 0)

    is_ctx = _is_ctx_rows(tm, tiles_per_batch)
    r = route_ref[...]
    y = r[:, 2:3] * _from_token_tiles(b0_ref, tm, n_seg) + r[:, 3:4] * _from_token_tiles(b1_ref, tm, n_seg)
    o_ref[...] = x_ref[...] + _mod_row(mod_ref, 5, is_ctx) * _rms(y, g_ref[...])


def _combine(slots, x, ys, route, mod, g3, *, tm, tiles_per_batch):
    n, d = x.shape
    n_seg = d // LANES
    kern = functools.partial(_combine_kernel, tm=tm, tiles_per_batch=tiles_per_batch, n_tokens=n, n_seg=n_seg)
    return pl.pallas_call(
        kern,
        out_shape=jax.ShapeDtypeStruct((n, d), F32),
        grid_spec=pltpu.PrefetchScalarGridSpec(
            num_scalar_prefetch=1,
            grid=(n // tm,),
            in_specs=[
                pl.BlockSpec((tm, d), lambda i, s: (i, 0)),
                pl.BlockSpec(memory_space=pl.ANY),
                pl.BlockSpec((tm, LANES), lambda i, s: (i, 0)),
                pl.BlockSpec((None, 2 * SUBLANES, d), lambda i, s: (i // tiles_per_batch, 0, 0)),
                pl.BlockSpec((1, d), lambda i, s: (0, 0)),
            ],
            out_specs=pl.BlockSpec((tm, d), lambda i, s: (i, 0)),
            scratch_shapes=[pltpu.VMEM((tm * n_seg, LANES), F32), pltpu.VMEM((tm * n_seg, LANES), F32),
                            pltpu.SemaphoreType.DMA],
        ),
        compiler_params=_cparams(("arbitrary",)),
        name="expert_combine",
    )(slots, x, ys, route, mod, g3)


def _moe(x1, h_tiles, route, w1, w3, w2, mod, g3, *, tm, tiles_per_batch):
    n, d = x1.shape
    rank, counts = _rank(route, tm=tm)
    counts = counts[0, :N_EXPERTS].astype(jnp.int32)
    padded = ((counts + EXPERT_TILE - 1) // EXPERT_TILE) * EXPERT_TILE
    ends = jnp.cumsum(padded)
    offs = ends - padded
    e12 = route[:, 0:2].astype(jnp.int32)
    slot12 = offs[e12] + rank[:, 0:2].astype(jnp.int32)
    slots = jnp.concatenate([slot12[:, 0], slot12[:, 1]])
    capacity = 2 * n + N_EXPERTS * EXPERT_TILE
    pads = jnp.concatenate([offs + counts, ends.at[-1].set(capacity)]).astype(jnp.int32)
    n_tiles = capacity // EXPERT_TILE
    tile_start = jnp.arange(n_tiles, dtype=jnp.int32) * EXPERT_TILE
    tile_expert = jnp.minimum(jnp.sum(tile_start[:, None] >= ends[None, :], axis=1), N_EXPERTS - 1).astype(jnp.int32)
    n_active = (ends[-1:] // EXPERT_TILE).astype(jnp.int32)

    xs = _dispatch(slots, pads, h_tiles, capacity=capacity, tm=tm, n_tokens=n)
    ys = _experts(tile_expert, n_active, xs, w1, w3, w2, d=d)
    return _combine(slots, x1, ys, route, mod, g3, tm=tm, tiles_per_batch=tiles_per_batch)


def _rope_tables(s_len, dim, lane_lo, block):
    t = jnp.arange(s_len, dtype=jnp.int32)
    row, col = t // GRID_W, t % GRID_W
    n_freq = dim // 4
    inv = 1.0 / (ROPE_BASE ** (jnp.arange(n_freq, dtype=F32) / n_freq))
    ang = jnp.concatenate([row.astype(F32)[:, None] * inv, col.astype(F32)[:, None] * inv], axis=-1)
    cos, sin = jnp.cos(ang), jnp.sin(ang)
    lane = jnp.arange(LANES)
    rel = (lane - lane_lo) % block
    in_rope = (lane >= lane_lo) & (rel < dim)
    idx = rel % (dim // 2)
    first = rel < dim // 2
    c = jnp.where(in_rope[None], cos[:, idx], 1.0)
    s1 = jnp.where((in_rope & first)[None], -sin[:, idx], 0.0)
    s2 = jnp.where((in_rope & ~first)[None], sin[:, idx], 0.0)
    pad = lambda a, v: jnp.concatenate([jnp.full((CTX_LEN, LANES), v, F32), a], axis=0)
    return pad(c, 1.0), pad(s1, 0.0), pad(s2, 0.0)


def _mla_weights(w_uq, w_ukv):
    dq = D_NOPE + D_ROPE
    wq = jnp.zeros((Q_LORA, D_HEADS, LANES), F32).at[:, :, :dq].set(w_uq.reshape(Q_LORA, D_HEADS, dq))
    kv = w_ukv.reshape(KV_LORA, D_HEADS, D_NOPE + D_VDIM)
    wk = jnp.zeros((2 * KV_LORA, D_HEADS, LANES), F32)
    wk = wk.at[:KV_LORA, :, :D_NOPE].set(kv[:, :, :D_NOPE])
    place = jnp.broadcast_to(jnp.eye(D_ROPE, dtype=F32)[:, None, :], (D_ROPE, D_HEADS, D_ROPE))
    wk = wk.at[KV_LORA:KV_LORA + D_ROPE, :, D_NOPE:dq].set(place)
    wv = kv[:, :, D_NOPE:]
    hw = D_HEADS * LANES
    return (wq.reshape(Q_LORA, hw).astype(BF16), wk.reshape(2 * KV_LORA, hw).astype(BF16),
            wv.reshape(KV_LORA, hw).astype(BF16))


def kernel(x, c, ctx, c_ctx, w_mod, b_mod, norm_g, ev_w_in, ev_w_out, sgu_w, sgu_b, sgu_g, diff_lq1, diff_lk1, diff_lq2, diff_lk2, diff_subln_g, ffn_w1, ffn_w3, ffn_w2, od_w_in, od_w_out, na_rpb, mla_gq, mla_w_uq, mla_gkv, mla_w_ukv, router_w, moe_w1, moe_w3, moe_w2):
    bsz, s_len, d = x.shape
    depth = w_mod.shape[0]
    assert ctx.shape[1] == CTX_LEN and bsz + 1 <= SUBLANES and s_len % (NA_GROUP_ROWS * GRID_W) == 0
    t = CTX_LEN + s_len
    n = bsz * t
    tm = 768 if t % 768 == 0 else 256
    tiles_per_batch = t // tm
    grid_rows = s_len // GRID_W

    cond = jnp.zeros((SUBLANES, d), F32).at[:bsz].set(c).at[bsz].set(c_ctx)
    mod = _modulation(cond, w_mod, b_mod).reshape(depth, SUBLANES, N_MOD, d)
    pad2 = jnp.zeros((depth, bsz, SUBLANES - N_MOD, d), F32)
    mod = jnp.concatenate(
        [mod[:, :bsz], pad2, jnp.broadcast_to(mod[:, bsz:bsz + 1], (depth, bsz, N_MOD, d)), pad2], axis=2)

    rope_b = _rope_tables(s_len, B_HEAD_DIM, 0, B_HEAD_DIM)
    rope_d = _rope_tables(s_len, D_ROPE, D_NOPE, LANES)
    no_lam = jnp.zeros((1,), F32)
    no_gain = jnp.ones((1, LANES), F32)

    xs = jnp.concatenate([ctx, x], axis=1).reshape(n, d)
    kw = dict(tm=tm, tiles_per_batch=tiles_per_batch)
    for l in range(depth):
        i = l // 2
        g = norm_g[l].reshape(4, 1, d)
        if l % 2 == 0:
            lam_init = 0.8 - 0.6 * math.exp(-0.3 * l)
            lam = (jnp.exp(jnp.sum(diff_lq1[i] * diff_lk1[i])) - jnp.exp(jnp.sum(diff_lq2[i] * diff_lk2[i]))
                   + lam_init).reshape(1).astype(F32)
            p = _pre_even(xs, mod[l], g[0], ev_w_in[i].astype(BF16), sgu_w[i].astype(BF16), sgu_b[i].T,
                          sgu_g[i].reshape(1, A_WIDTH), rope_b, **kw)
            nb = A_WIDTH // LANES
            o = _attention(p.reshape(bsz, t, -1), lam, diff_subln_g[i].reshape(1, LANES), n_heads=B_HEADS,
                           n_maps=2, qcol=nb, kcol=nb + B_HEADS, vcol=nb + 2 * B_HEADS,
                           post_norm=True, out_scale=1.0 - lam_init)
            x1, h2 = _post(xs, p, o.reshape(n, -1), ev_w_out[i].astype(BF16), mod[l], g[1], g[2], a_col=0, **kw)
            xs = _ffn(x1, h2, ffn_w1[i].astype(BF16), ffn_w3[i].astype(BF16), ffn_w2[i].astype(BF16),
                      mod[l], g[3], **kw)
        else:
            w_in = jnp.pad(od_w_in[i], ((0, 0), (0, LANES - D_ROPE))).astype(BF16)
            wuq, wk, wv = _mla_weights(mla_w_uq[i], mla_w_ukv[i])
            p = _pre_odd(xs, mod[l], g[0], w_in, mla_gq[i].reshape(1, Q_LORA), mla_gkv[i].reshape(1, KV_LORA),
                         wuq, wk, wv, rope_d, **kw)
            p3 = p.reshape(bsz, t, -1)
            o_c = _neighbourhood_attention(p3, _na_bias_tables(na_rpb[i], grid_rows))
            nb = 3 * C_WIDTH // LANES
            o_d = _attention(p3, no_lam, no_gain, n_heads=D_HEADS, n_maps=1, qcol=nb, kcol=nb + D_HEADS,
                             vcol=nb + 2 * D_HEADS, post_norm=False, out_scale=1.0)
            router = jnp.pad(router_w[i], ((0, 0), (0, LANES - N_EXPERTS))).astype(BF16)
            x1, h_tiles, route = _post(xs, o_c.reshape(n, -1), o_d.reshape(n, -1), od_w_out[i].astype(BF16),
                                       mod[l], g[1], g[2], router, a_col=0, **kw)
            xs = _moe(x1, h_tiles, route, moe_w1[i].astype(BF16), moe_w3[i].astype(BF16),
                      moe_w2[i].astype(BF16), mod[l], g[3], **kw)
    return xs.reshape(bsz, t, d)[:, CTX_LEN:]
```
